```python
import math
import jax, jax.numpy as jnp
from jax import lax
import numpy as np

D_MODEL = 1024
BATCH = 4
SEQ = 8192
DEPTH = 2

GRID_W = 64
CTX_LEN = 256
HEAD_DIM = 64
ATT_WIDTH = D_MODEL // 2
ATT_HEADS = ATT_WIDTH // HEAD_DIM
ATT_KV_HEADS = ATT_HEADS // 4
KV_WIDTH = ATT_KV_HEADS * HEAD_DIM
ROPE_HALF = HEAD_DIM // 2
ROPE_AXIS_FREQS = HEAD_DIM // 4
ROPE_THETA = 10000.0
Q_BLOCK = 128
LRU_WIDTH = D_MODEL // 4
LRU_BLOCKS = 4
LRU_BLOCK_W = LRU_WIDTH // LRU_BLOCKS
LRU_C = 8.0
LRU_CONV = 4
LRU_CONV_LEFT = 2
HY_WIDTH = D_MODEL // 4
HY_CONV = 3
HY_CONV_LEFT = 1
HY_EMB = 33
HY_BANDS = (HY_EMB - 1) // 2
HY_FFN = 64
HY_FAST_DECAY = 0.3
HY_SLOW_DECAY = 1.5
HY_DECAY_TARGET = 1e-2
MIX_WIDTH = ATT_WIDTH + LRU_WIDTH + HY_WIDTH
IN_WIDTH = ATT_WIDTH + 2 * KV_WIDTH + 2 * LRU_WIDTH + 3 * HY_WIDTH
N_EXPERTS = 32
TOP_K = 4
D_FF = D_MODEL
SWIGLU_ALPHA = 1.702
SWIGLU_LIMIT = 7.0
EPS = 1e-6

kernel_name = 'hymba_gqa_rglru_hyena_moe_prefix_dit'

F32 = jnp.float32


def rmsnorm(x, g):
    xf = x.astype(F32)
    y = xf * lax.rsqrt(jnp.mean(xf * xf, axis=-1, keepdims=True) + EPS)
    return (y * g.astype(F32)).astype(x.dtype)


def modulate(h, shift, scale):
    return h * (1 + scale) + shift


def grid_rope(L):
    rows_n = L // GRID_W
    rows = jnp.repeat(jnp.arange(rows_n), GRID_W).astype(F32)
    cols = jnp.tile(jnp.arange(GRID_W), rows_n).astype(F32)
    inv = ROPE_THETA ** (-jnp.arange(ROPE_AXIS_FREQS, dtype=F32) / ROPE_AXIS_FREQS)
    ang = jnp.concatenate([rows[:, None] * inv, cols[:, None] * inv], axis=-1)
    return jnp.cos(ang)[:, None, :], jnp.sin(ang)[:, None, :]


def apply_rope(x, cos, sin):
    xf = x.astype(F32)
    x1, x2 = xf[..., :ROPE_HALF], xf[..., ROPE_HALF:]
    return jnp.concatenate([x1 * cos - x2 * sin, x2 * cos + x1 * sin], axis=-1).astype(x.dtype)


def dwconv(x, w, b, left):
    W, C = w.shape
    y = lax.conv_general_dilated(x, w[:, None, :].astype(x.dtype), window_strides=(1,),
                                 padding=[(left, W - 1 - left)],
                                 dimension_numbers=('NWC', 'WIO', 'NWC'), feature_group_count=C)
    return y + b.astype(x.dtype)


def split_proj(u):
    sizes = (ATT_WIDTH, KV_WIDTH, KV_WIDTH, LRU_WIDTH, LRU_WIDTH, 3 * HY_WIDTH)
    idx = np.cumsum(sizes)[:-1].tolist()
    return jnp.split(u, idx, axis=-1)


def gqa(q, k, v):
    s = jnp.einsum('bqkgd,bskd->bkgqs', q, k).astype(F32) * (HEAD_DIM ** -0.5)
    p = jax.nn.softmax(s, axis=-1).astype(v.dtype)
    return jnp.einsum('bkgqs,bskd->bqkgd', p, v)


def lru_coeffs(xc, w, b, lam):
    B, L, C = xc.shape
    xf = xc.astype(F32)
    g = jnp.einsum('blnd,jnde->jblne', xf.reshape(B, L, LRU_BLOCKS, LRU_BLOCK_W), w.astype(F32))
    g = g.reshape(2, B, L, C) + b.astype(F32)[:, None, None, :]
    r = jax.nn.sigmoid(g[0])
    i = jax.nn.sigmoid(g[1])
    log_a = -LRU_C * r * jax.nn.softplus(-lam.astype(F32))
    a = jnp.exp(log_a)
    bx = jnp.sqrt(-jnp.expm1(2.0 * log_a)) * (i * xf)
    return a, bx


def linear_scan(a, bx, h0, reverse):
    idx = -1 if reverse else 0
    bx = bx.at[:, idx].add(a[:, idx] * h0)

    def comb(left, right):
        a_l, b_l = left
        a_r, b_r = right
        return a_l * a_r, a_r * b_l + b_r

    _, h = lax.associative_scan(comb, (a, bx), axis=1, reverse=reverse)
    return h


def hyena_kernel(L, p):
    t01 = jnp.linspace(0.0, 1.0, L, dtype=F32)[:, None]
    bands = jnp.linspace(1e-4, HY_BANDS - 1, HY_BANDS, dtype=F32)
    ang = (2.0 * math.pi / L) * jnp.arange(L, dtype=F32)[:, None] * bands
    z = jnp.concatenate([t01, jnp.cos(ang), -jnp.sin(ang)], axis=-1)
    fr = p['hy_freq'].astype(F32)
    h = jnp.sin(fr * (z @ p['hy_w1'].astype(F32) + p['hy_b1'].astype(F32)))
    h = jnp.sin(fr * (h @ p['hy_w2'].astype(F32) + p['hy_b2'].astype(F32)))
    h = (h @ p['hy_w3'].astype(F32)) * jnp.exp(-t01 * jnp.abs(p['hy_decay'].astype(F32)))
    hf, hb = h[:, :HY_WIDTH], h[:, HY_WIDTH:]
    k = jnp.concatenate([hf, jnp.zeros((1, HY_WIDTH), F32), jnp.flip(hb[1:], axis=0)], axis=0)
    return k / jnp.sum(jnp.abs(k), axis=0, keepdims=True)


def hyena(u, p):
    L = u.shape[1]
    uc = dwconv(u, p['hy_cw'], p['hy_cb'], HY_CONV_LEFT)
    x0, x1, v = jnp.split(uc, 3, axis=-1)
    k = hyena_kernel(L, p)
    v = (v * x1).astype(F32)
    n = 2 * L
    y = jnp.fft.irfft(jnp.fft.rfft(v, n=n, axis=1) * jnp.fft.rfft(k, n=n, axis=0)[None], n=n, axis=1)[:, :L]
    y = y + v * p['hy_bias'].astype(F32)
    return y.astype(u.dtype) * x0


def merge_groups(att, lru, hyo, p):
    g = p['out_g']
    o = jnp.concatenate([rmsnorm(att, g[:ATT_WIDTH]),
                         rmsnorm(lru, g[ATT_WIDTH:ATT_WIDTH + LRU_WIDTH]),
                         rmsnorm(hyo, g[ATT_WIDTH + LRU_WIDTH:])], axis=-1)
    return o @ p['w_out']


def mixer(h_lat, h_ctx, p, ctx_out):
    B, S, _ = h_lat.shape
    Lc = h_ctx.shape[1]
    G = ATT_HEADS // ATT_KV_HEADS
    q, k, v, lx, lg, hy = split_proj(h_lat @ p['w_in'])
    qc, kc, vc, lxc, lgc, hyc = split_proj(h_ctx @ p['w_in'])
    cos, sin = grid_rope(S)
    q = apply_rope(rmsnorm(q.reshape(B, S, ATT_HEADS, HEAD_DIM), p['q_g']), cos, sin)
    k = apply_rope(rmsnorm(k.reshape(B, S, ATT_KV_HEADS, HEAD_DIM), p['k_g']), cos, sin)
    v = v.reshape(B, S, ATT_KV_HEADS, HEAD_DIM)
    kc = rmsnorm(kc.reshape(B, Lc, ATT_KV_HEADS, HEAD_DIM), p['k_g'])
    vc = vc.reshape(B, Lc, ATT_KV_HEADS, HEAD_DIM)
    k_all = jnp.concatenate([kc, k], axis=1)
    v_all = jnp.concatenate([vc, v], axis=1)
    qb = q.reshape(B, S // Q_BLOCK, Q_BLOCK, ATT_KV_HEADS, G, HEAD_DIM).transpose(1, 0, 2, 3, 4, 5)
    att = lax.map(lambda qi: gqa(qi, k_all, v_all), qb)
    att = att.transpose(1, 0, 2, 3, 4, 5).reshape(B, S, ATT_WIDTH)
    xl = dwconv(lx, p['lru_cw'], p['lru_cb'], LRU_CONV_LEFT)
    xcx = dwconv(lxc, p['lru_cw'], p['lru_cb'], LRU_CONV_LEFT)
    h_lat_dirs = []
    h_ctx_dirs = []
    for d, rev in enumerate((False, True)):
        a_c, b_c = lru_coeffs(xcx, p['lru_gw'][d], p['lru_gb'][d], p['lru_lam'][d])
        h_c = linear_scan(a_c, b_c, jnp.zeros_like(b_c[:, 0]), rev)
        h_end = h_c[:, 0] if rev else h_c[:, -1]
        a_l, b_l = lru_coeffs(xl, p['lru_gw'][d], p['lru_gb'][d], p['lru_lam'][d])
        h_lat_dirs.append(linear_scan(a_l, b_l, h_end, rev))
        h_ctx_dirs.append(h_c)
    lru = (h_lat_dirs[0] + h_lat_dirs[1]).astype(h_lat.dtype) * jax.nn.gelu(lg)
    hyo = hyena(hy, p)
    o_lat = merge_groups(att, lru, hyo, p)
    if not ctx_out:
        return o_lat, None
    qc = rmsnorm(qc.reshape(B, Lc, ATT_KV_HEADS, G, HEAD_DIM), p['q_g'])
    att_c = gqa(qc, kc, vc).reshape(B, Lc, ATT_WIDTH)
    lru_c = (h_ctx_dirs[0] + h_ctx_dirs[1]).astype(h_ctx.dtype) * jax.nn.gelu(lgc)
    o_ctx = merge_groups(att_c, lru_c, hyena(hyc, p), p)
    return o_lat, o_ctx


def moe(h, router_w, router_b, w1, b1, w2, b2):
    T = h.shape[0]
    logits = (h @ router_w).astype(F32) + router_b.astype(F32)
    top_v, top_i = lax.top_k(logits, TOP_K)
    gate = jax.nn.softmax(top_v, axis=-1)
    dense_gate = jnp.sum(jax.nn.one_hot(top_i, N_EXPERTS, dtype=F32) * gate[..., None], axis=1)
    out = jnp.zeros((T, h.shape[1]), F32)
    for e in range(N_EXPERTS):
        hu = (h @ w1[e] + b1[e]).astype(F32)
        glu = jnp.minimum(hu[:, ::2], SWIGLU_LIMIT)
        lin = jnp.clip(hu[:, 1::2], -SWIGLU_LIMIT, SWIGLU_LIMIT)
        act = (glu * jax.nn.sigmoid(SWIGLU_ALPHA * glu) * (lin + 1.0)).astype(h.dtype)
        out = out + dense_gate[:, e:e + 1] * (act @ w2[e] + b2[e]).astype(F32)
    return out.astype(h.dtype)


def setup_inputs(seed: int = 0) -> dict:
    key = jax.random.key(seed)
    ks = iter(list(jax.random.split(key, 40)))

    def nrm(shape, scale):
        return jax.random.normal(next(ks), shape, F32) * scale

    D = D_MODEL
    x = nrm((BATCH, SEQ, D), 1.0)
    c = nrm((BATCH, D), 1.0)
    ctx = nrm((BATCH, CTX_LEN, D), 1.0)
    c_ctx = nrm((D,), 1.0)
    w_mod = nrm((DEPTH, D, 6 * D), 0.5 * D ** -0.5)
    b_mod = nrm((DEPTH, 6 * D), 0.02)
    norm_mix_g = 1.0 + nrm((DEPTH, D), 0.05)
    norm_ffn_g = 1.0 + nrm((DEPTH, D), 0.05)
    w_in = nrm((DEPTH, D, IN_WIDTH), D ** -0.5)
    q_norm_g = 1.0 + nrm((DEPTH, HEAD_DIM), 0.05)
    k_norm_g = 1.0 + nrm((DEPTH, HEAD_DIM), 0.05)
    lru_conv_w = nrm((DEPTH, LRU_CONV, LRU_WIDTH), LRU_CONV ** -0.5)
    lru_conv_b = nrm((DEPTH, LRU_WIDTH), 0.02)
    lru_gate_w = nrm((DEPTH, 2, 2, LRU_BLOCKS, LRU_BLOCK_W, LRU_BLOCK_W), LRU_BLOCK_W ** -0.5)
    lru_gate_b = nrm((DEPTH, 2, 2, LRU_WIDTH), 0.02)
    a_pow = jax.random.uniform(next(ks), (DEPTH, 2, LRU_WIDTH), F32, minval=0.9, maxval=0.999)
    a_base = a_pow ** (1.0 / LRU_C)
    lru_lambda = jnp.log(a_base) - jnp.log1p(-a_base)
    hy_conv_w = nrm((DEPTH, HY_CONV, 3 * HY_WIDTH), HY_CONV ** -0.5)
    hy_conv_b = nrm((DEPTH, 3 * HY_WIDTH), 0.02)
    hy_w1 = nrm((DEPTH, HY_EMB, HY_FFN), HY_EMB ** -0.5)
    hy_b1 = nrm((DEPTH, HY_FFN), 0.02)
    hy_w2 = nrm((DEPTH, HY_FFN, HY_FFN), HY_FFN ** -0.5)
    hy_b2 = nrm((DEPTH, HY_FFN), 0.02)
    hy_w3 = nrm((DEPTH, HY_FFN, 2 * HY_WIDTH), HY_FFN ** -0.5)
    hy_freq = 1.0 + nrm((DEPTH, HY_FFN), 0.05)
    min_decay = math.log(HY_DECAY_TARGET) / HY_SLOW_DECAY
    max_decay = math.log(HY_DECAY_TARGET) / HY_FAST_DECAY
    base_decay = jnp.tile(jnp.linspace(min_decay, max_decay, HY_WIDTH, dtype=F32), 2)
    hy_decay = base_decay[None, :] + nrm((DEPTH, 2 * HY_WIDTH), 0.1)
    hy_bias = nrm((DEPTH, HY_WIDTH), 0.5)
    out_norm_g = 1.0 + nrm((DEPTH, MIX_WIDTH), 0.05)
    w_out = nrm((DEPTH, MIX_WIDTH, D), MIX_WIDTH ** -0.5)
    router_w = nrm((DEPTH, D, N_EXPERTS), D ** -0.5)
    router_b = nrm((DEPTH, N_EXPERTS), 0.01)
    moe_w1 = nrm((DEPTH, N_EXPERTS, D, 2 * D_FF), D ** -0.5)
    moe_b1 = nrm((DEPTH, N_EXPERTS, 2 * D_FF), 0.01)
    moe_w2 = nrm((DEPTH, N_EXPERTS, D_FF, D), D_FF ** -0.5)
    moe_b2 = nrm((DEPTH, N_EXPERTS, D), 0.01)
    return {'x': x, 'c': c, 'ctx': ctx, 'c_ctx': c_ctx, 'w_mod': w_mod, 'b_mod': b_mod,
            'norm_mix_g': norm_mix_g, 'norm_ffn_g': norm_ffn_g, 'w_in': w_in,
            'q_norm_g': q_norm_g, 'k_norm_g': k_norm_g, 'lru_conv_w': lru_conv_w, 'lru_conv_b': lru_conv_b,
            'lru_gate_w': lru_gate_w, 'lru_gate_b': lru_gate_b, 'lru_lambda': lru_lambda,
            'hy_conv_w': hy_conv_w, 'hy_conv_b': hy_conv_b, 'hy_w1': hy_w1, 'hy_b1': hy_b1,
            'hy_w2': hy_w2, 'hy_b2': hy_b2, 'hy_w3': hy_w3, 'hy_freq': hy_freq, 'hy_decay': hy_decay,
            'hy_bias': hy_bias, 'out_norm_g': out_norm_g, 'w_out': w_out, 'router_w': router_w,
            'router_b': router_b, 'moe_w1': moe_w1, 'moe_b1': moe_b1, 'moe_w2': moe_w2, 'moe_b2': moe_b2}


def reference(x, c, ctx, c_ctx, w_mod, b_mod, norm_mix_g, norm_ffn_g, w_in, q_norm_g, k_norm_g,
              lru_conv_w, lru_conv_b, lru_gate_w, lru_gate_b, lru_lambda, hy_conv_w, hy_conv_b,
              hy_w1, hy_b1, hy_w2, hy_b2, hy_w3, hy_freq, hy_decay, hy_bias, out_norm_g, w_out,
              router_w, router_b, moe_w1, moe_b1, moe_w2, moe_b2):
    B, S, D = x.shape
    Lc = ctx.shape[1]
    for l in range(DEPTH):
        last = l == DEPTH - 1
        mod = jax.nn.silu(c) @ w_mod[l] + b_mod[l]
        mod_c = jax.nn.silu(c_ctx) @ w_mod[l] + b_mod[l]
        sa, ca, ga, sf, cf, gf = jnp.split(mod[:, None, :], 6, axis=-1)
        sa_c, ca_c, ga_c, sf_c, cf_c, gf_c = jnp.split(mod_c, 6, axis=-1)
        p = {'w_in': w_in[l], 'q_g': q_norm_g[l], 'k_g': k_norm_g[l],
             'lru_cw': lru_conv_w[l], 'lru_cb': lru_conv_b[l], 'lru_gw': lru_gate_w[l],
             'lru_gb': lru_gate_b[l], 'lru_lam': lru_lambda[l],
             'hy_cw': hy_conv_w[l], 'hy_cb': hy_conv_b[l], 'hy_w1': hy_w1[l], 'hy_b1': hy_b1[l],
             'hy_w2': hy_w2[l], 'hy_b2': hy_b2[l], 'hy_w3': hy_w3[l], 'hy_freq': hy_freq[l],
             'hy_decay': hy_decay[l], 'hy_bias': hy_bias[l], 'out_g': out_norm_g[l], 'w_out': w_out[l]}
        h_lat = modulate(rmsnorm(x, norm_mix_g[l]), sa, ca)
        h_ctx = modulate(rmsnorm(ctx, norm_mix_g[l]), sa_c, ca_c)
        o_lat, o_ctx = mixer(h_lat, h_ctx, p, not last)
        x = x + ga * o_lat
        f_lat = modulate(rmsnorm(x, norm_ffn_g[l]), sf, cf).reshape(B * S, D)
        if last:
            y = moe(f_lat, router_w[l], router_b[l], moe_w1[l], moe_b1[l], moe_w2[l], moe_b2[l])
            x = x + gf * y.reshape(B, S, D)
        else:
            ctx = ctx + ga_c * o_ctx
            f_ctx = modulate(rmsnorm(ctx, norm_ffn_g[l]), sf_c, cf_c).reshape(B * Lc, D)
            y = moe(jnp.concatenate([f_lat, f_ctx], axis=0), router_w[l], router_b[l],
                    moe_w1[l], moe_b1[l], moe_w2[l], moe_b2[l])
            x = x + gf * y[:B * S].reshape(B, S, D)
            ctx = ctx + gf_c * y[B * S:].reshape(B, Lc, D)
    return x
```

```python
import functools
import math

import jax
import jax.numpy as jnp
from jax import lax
from jax.experimental import pallas as pl
from jax.experimental.pallas import tpu as pltpu

F32 = jnp.float32
BF16 = jnp.bfloat16
I32 = jnp.int32

D_MODEL = 1024
HEAD_DIM = 64
ATT_WIDTH = 512
KV_WIDTH = 128
LRU_WIDTH = 256
HY_WIDTH = 256
REST_WIDTH = 2 * LRU_WIDTH + 3 * HY_WIDTH
IN_WIDTH = ATT_WIDTH + 2 * KV_WIDTH + REST_WIDTH
GRID_W = 64
ROPE_THETA = 10000.0
LRU_C = 8.0
HY_EMB = 33
HY_BANDS = 16
HY_FFN = 64
N_EXPERTS = 32
TOP_K = 4
D_FF = 1024
SWIGLU_ALPHA = 1.702
SWIGLU_LIMIT = 7.0
EPS = 1e-6
LANES = 128
SUBLANES = 8
DFT_N2 = 128
VMEM_LIMIT = 56 * 1024 * 1024


def _cparams(*sem):
    return pltpu.CompilerParams(dimension_semantics=sem, vmem_limit_bytes=VMEM_LIMIT)


def _dot(a, b):
    return jnp.dot(a, b, preferred_element_type=F32)


def _split(x):
    hi = x.astype(BF16)
    return hi, (x - hi.astype(F32)).astype(BF16)


def _dot3(a, b):
    ah, al = _split(a)
    bh, bl = _split(b)
    return _dot(ah, bh) + (_dot(al, bh) + _dot(ah, bl))


def _sigmoid(x):
    return 1.0 / (1.0 + jnp.exp(-x))


def _rms(x, g):
    return x * lax.rsqrt(jnp.mean(x * x, axis=-1, keepdims=True) + EPS) * g


def _mod_kernel(c_ref, w_ref, b_ref, o_ref):
    c = c_ref[...]
    o_ref[0] = _dot3(c * _sigmoid(c), w_ref[0]) + b_ref[0]


def _mod_call(c8, w_mod, b_mod):
    depth, d, n6 = w_mod.shape
    tn = n6 // 4
    return pl.pallas_call(
        _mod_kernel,
        out_shape=jax.ShapeDtypeStruct((depth, SUBLANES, n6), F32),
        grid=(depth, n6 // tn),
        in_specs=[pl.BlockSpec((SUBLANES, d), lambda l, j: (0, 0)),
                  pl.BlockSpec((1, d, tn), lambda l, j: (l, 0, j)),
                  pl.BlockSpec((1, 1, tn), lambda l, j: (l, 0, j))],
        out_specs=pl.BlockSpec((1, SUBLANES, tn), lambda l, j: (l, 0, j)),
        compiler_params=_cparams("arbitrary", "arbitrary"),
        name="mod",
    )(c8, w_mod, b_mod.reshape(depth, 1, n6))


def _head_norm(x, g, bd):
    hi, lo = _split(x * x)
    ms = _dot(hi, bd) + _dot(lo, bd)
    return x * lax.rsqrt(ms + EPS) * g


def _rope(y, cos, sin, first_half):
    swapped = jnp.where(first_half, pltpu.roll(y, LANES - HEAD_DIM // 2, 1), pltpu.roll(y, HEAD_DIM // 2, 1))
    return y * cos + swapped * sin


def _in_proj_kernel(*refs, rope, ctx_row):
    if rope:
        x_ref, mod_ref, g_ref, w_ref, qg_ref, kg_ref, bd_ref, cos_ref, sin_ref, q_ref, k_ref, v_ref, r_ref = refs
    else:
        x_ref, mod_ref, g_ref, w_ref, qg_ref, kg_ref, bd_ref, q_ref, k_ref, v_ref, r_ref = refs
    row = pl.program_id(0) if ctx_row is None else ctx_row
    d = x_ref.shape[2]
    tm = x_ref.shape[1]
    shift = mod_ref[pl.ds(row, 1), 0:d]
    scale = mod_ref[pl.ds(row, 1), d:2 * d]
    h = _rms(x_ref[0], g_ref[...]) * (1.0 + scale) + shift
    u = _dot(h.astype(BF16), w_ref[...])
    bd = bd_ref[...]
    if rope:
        cos = cos_ref[...]
        sin = sin_ref[...]
        lane = lax.broadcasted_iota(I32, (tm, LANES), 1)
        first_half = (lane & (HEAD_DIM // 2)) == 0
    for p in range(ATT_WIDTH // LANES):
        y = _head_norm(u[:, p * LANES:(p + 1) * LANES], qg_ref[...], bd)
        if rope:
            y = _rope(y, cos, sin, first_half)
        q_ref[0, :, p * LANES:(p + 1) * LANES] = (y * (HEAD_DIM ** -0.5)).astype(BF16)
    y = _head_norm(u[:, ATT_WIDTH:ATT_WIDTH + KV_WIDTH], kg_ref[...], bd)
    if rope:
        y = _rope(y, cos, sin, first_half)
    k_ref[0] = y.astype(BF16)
    v_ref[0] = u[:, ATT_WIDTH + KV_WIDTH:ATT_WIDTH + 2 * KV_WIDTH].astype(BF16)
    r_ref[0] = u[:, ATT_WIDTH + 2 * KV_WIDTH:]


def _in_proj_call(x, mod_l, g, w_in, qg, kg, bd, rope_tabs, ctx_row, tm):
    b, l, d = x.shape
    rope = rope_tabs is not None
    full = lambda shape: pl.BlockSpec(shape, lambda bi, i: (0,) * len(shape))
    in_specs = [pl.BlockSpec((1, tm, d), lambda bi, i: (bi, i, 0)),
                full(mod_l.shape), full((1, d)), full(w_in.shape), full((1, LANES)), full((1, LANES)),
                full((LANES, LANES))]
    args = [x, mod_l, g, w_in, qg, kg, bd]
    if rope:
        in_specs += [pl.BlockSpec((tm, LANES), lambda bi, i: (i, 0))] * 2
        args += list(rope_tabs)
    tok = lambda w: pl.BlockSpec((1, tm, w), lambda bi, i: (bi, i, 0))
    return pl.pallas_call(
        functools.partial(_in_proj_kernel, rope=rope, ctx_row=ctx_row),
        out_shape=(jax.ShapeDtypeStruct((b, l, ATT_WIDTH), BF16), jax.ShapeDtypeStruct((b, l, KV_WIDTH), BF16),
                   jax.ShapeDtypeStruct((b, l, KV_WIDTH), BF16), jax.ShapeDtypeStruct((b, l, REST_WIDTH), F32)),
        grid=(b, l // tm),
        in_specs=in_specs,
        out_specs=(tok(ATT_WIDTH), tok(KV_WIDTH), tok(KV_WIDTH), tok(REST_WIDTH)),
        compiler_params=_cparams("arbitrary", "arbitrary"),
        name="in_proj_lat" if rope else "in_proj_ctx",
    )(*args)


def _conv_taps(win, w, left, tm):
    acc = None
    for j in range(w.shape[0]):
        off = SUBLANES + j - left
        term = win[off:off + tm] * w[j:j + 1]
        acc = term if acc is None else acc + term
    return acc


def _prep_kernel(cur_ref, prev_ref, next_ref, lw_ref, lb_ref, hw_ref, hb_ref, xl_ref, vv_ref, x0_ref):
    i = pl.program_id(1)
    tm = cur_ref.shape[1]
    has_prev = (i > 0).astype(F32)
    has_next = (i < pl.num_programs(1) - 1).astype(F32)

    def window(lo, hi):
        return jnp.concatenate([prev_ref[0, :, lo:hi] * has_prev, cur_ref[0, :, lo:hi],
                                next_ref[0, :, lo:hi] * has_next], axis=0)

    xl_ref[0] = _conv_taps(window(0, LRU_WIDTH), lw_ref[...], 2, tm) + lb_ref[...]
    uc = _conv_taps(window(2 * LRU_WIDTH, REST_WIDTH), hw_ref[...], 1, tm) + hb_ref[...]
    x0_ref[0] = uc[:, 0:HY_WIDTH]
    vv_ref[0] = uc[:, 2 * HY_WIDTH:] * uc[:, HY_WIDTH:2 * HY_WIDTH]


def _prep_call(rest, lw, lb, hw, hb, tm):
    b, l, w = rest.shape
    nb = tm // SUBLANES
    last = l // SUBLANES - 1
    full = lambda a: pl.BlockSpec(a.shape, lambda bi, i: (0,) * a.ndim)
    tok = pl.BlockSpec((1, tm, LRU_WIDTH), lambda bi, i: (bi, i, 0))
    out = jax.ShapeDtypeStruct((b, l, LRU_WIDTH), F32)
    return pl.pallas_call(
        _prep_kernel,
        out_shape=(out, out, out),
        grid=(b, l // tm),
        in_specs=[pl.BlockSpec((1, tm, w), lambda bi, i: (bi, i, 0)),
                  pl.BlockSpec((1, SUBLANES, w), lambda bi, i: (bi, jnp.maximum(i * nb - 1, 0), 0)),
                  pl.BlockSpec((1, SUBLANES, w), lambda bi, i: (bi, jnp.minimum((i + 1) * nb, last), 0)),
                  full(lw), full(lb), full(hw), full(hb)],
        out_specs=(tok, tok, tok),
        compiler_params=_cparams("arbitrary", "arbitrary"),
        name="conv_prep",
    )(rest, rest, rest, lw, lb, hw, hb)


def _attn_kernel(q_ref, k_ref, v_ref, o_ref, m_sc, l_sc, acc_sc, *, tk):
    tq = q_ref.shape[1]
    lk = k_ref.shape[2]
    lane = lax.broadcasted_iota(I32, (tq, LANES), 1)
    low = lane < HEAD_DIM
    for g in range(2):
        parts = []
        for p in (2 * g, 2 * g + 1):
            qs = q_ref[0, :, p * LANES:(p + 1) * LANES]
            zero = jnp.zeros_like(qs)
            parts += [jnp.where(low, qs, zero), jnp.where(low, zero, qs)]
        qst = jnp.concatenate(parts, axis=0)
        m_sc[...] = jnp.full(m_sc.shape, -jnp.inf, F32)
        l_sc[...] = jnp.zeros(l_sc.shape, F32)
        acc_sc[...] = jnp.zeros(acc_sc.shape, F32)

        def body(j, carry, g=g, qst=qst):
            start = pl.multiple_of(j * tk, tk)
            kc = k_ref[0, g, pl.ds(start, tk), :]
            vc = v_ref[0, g, pl.ds(start, tk), :]
            s = lax.dot_general(qst, kc, (((1,), (1,)), ((), ())), preferred_element_type=F32)
            m_old = m_sc[...]
            m_new = jnp.maximum(m_old, jnp.max(s, axis=1, keepdims=True))
            alpha = jnp.exp(m_old - m_new)
            p_ = jnp.exp(s - m_new)
            l_sc[...] = alpha * l_sc[...] + jnp.sum(p_, axis=1, keepdims=True)
            acc_sc[...] = alpha * acc_sc[...] + _dot(p_.astype(BF16), vc)
            m_sc[...] = m_new
            return carry

        lax.fori_loop(0, lk // tk, body, 0)
        o = acc_sc[...] / l_sc[...]
        for n, p in enumerate((2 * g, 2 * g + 1)):
            o_ref[0, :, p * LANES:(p + 1) * LANES] = jnp.where(
                low, o[2 * n * tq:(2 * n + 1) * tq], o[(2 * n + 1) * tq:(2 * n + 2) * tq])


def _attn_call(q, kd, vd, tq, tk):
    b, l, _ = q.shape
    lk = kd.shape[2]
    kv = pl.BlockSpec((1, 2, lk, LANES), lambda bi, i: (bi, 0, 0, 0))
    return pl.pallas_call(
        functools.partial(_attn_kernel, tk=tk),
        out_shape=jax.ShapeDtypeStruct((b, l, ATT_WIDTH), F32),
        grid=(b, l // tq),
        in_specs=[pl.BlockSpec((1, tq, ATT_WIDTH), lambda bi, i: (bi, i, 0)), kv, kv],
        out_specs=pl.BlockSpec((1, tq, ATT_WIDTH), lambda bi, i: (bi, i, 0)),
        scratch_shapes=[pltpu.VMEM((4 * tq, 1), F32), pltpu.VMEM((4 * tq, 1), F32),
                        pltpu.VMEM((4 * tq, LANES), F32)],
        compiler_params=_cparams("arbitrary", "arbitrary"),
        name="attn",
    )(q, kd, vd)


def _dup_heads(x):
    halves = [jnp.concatenate([x[..., g * HEAD_DIM:(g + 1) * HEAD_DIM]] * 2, axis=-1) for g in range(2)]
    return jnp.stack(halves, axis=1)


def _gelu_tanh(x):
    return 0.5 * x * (1.0 + jnp.tanh(math.sqrt(2.0 / math.pi) * (x + 0.044715 * (x * x * x))))


def _lru_kernel(*refs, reverse):
    if reverse:
        xl_ref, w_ref, b_ref, lam_ref, h0_ref, hf_ref, lg_ref, h_ref, o_ref, a_sc, b_sc, c_sc = refs
    else:
        xl_ref, w_ref, b_ref, lam_ref, h0_ref, h_ref, a_sc, b_sc, c_sc = refs
    tc = xl_ref.shape[1]
    nt = tc // SUBLANES

    @pl.when(pl.program_id(1) == 0)
    def _():
        c_sc[...] = jnp.broadcast_to(h0_ref[0], c_sc.shape)

    xc = xl_ref[0]
    gates = _dot(xc.astype(BF16), w_ref[...]) + b_ref[...]
    r = _sigmoid(gates[:, 0:LRU_WIDTH])
    gi = _sigmoid(gates[:, LRU_WIDTH:])
    nlam = -lam_ref[...]
    softplus = jnp.maximum(nlam, 0.0) + jnp.log1p(jnp.exp(-jnp.abs(nlam)))
    log_a = -LRU_C * r * softplus
    a = jnp.exp(log_a)
    a_sc[...] = a
    b_sc[...] = jnp.sqrt(1.0 - a * a) * (gi * xc)
    row = lax.broadcasted_iota(I32, (SUBLANES, LRU_WIDTH), 0)

    def tile(k, carry):
        kk = nt - 1 - k if reverse else k
        start = pl.multiple_of(kk * SUBLANES, SUBLANES)
        a = a_sc[pl.ds(start, SUBLANES), :]
        bv = b_sc[pl.ds(start, SUBLANES), :]
        for dist in (1, 2, 4):
            shift = SUBLANES - dist if reverse else dist
            keep = (row < SUBLANES - dist) if reverse else (row >= dist)
            a_n = pltpu.roll(a, shift, 0)
            b_n = pltpu.roll(bv, shift, 0)
            bv = jnp.where(keep, a * b_n + bv, bv)
            a = jnp.where(keep, a * a_n, a)
        h = a * c_sc[...] + bv
        h_ref[0, pl.ds(start, SUBLANES), :] = h
        edge = h[0:1] if reverse else h[SUBLANES - 1:SUBLANES]
        c_sc[...] = jnp.broadcast_to(edge, c_sc.shape)
        return carry

    lax.fori_loop(0, nt, tile, 0)
    if reverse:
        o_ref[0] = (hf_ref[0] + h_ref[0]) * _gelu_tanh(lg_ref[0])


def _lru_call(xl, w, bias, lam, h0, reverse, tc, hf=None, rest=None):
    b, l, c = xl.shape
    nc = l // tc
    chunk = (lambda bi, i: (bi, nc - 1 - i, 0)) if reverse else (lambda bi, i: (bi, i, 0))
    full = lambda a: pl.BlockSpec(a.shape, lambda bi, i: (0,) * a.ndim)
    tok = pl.BlockSpec((1, tc, c), chunk)
    in_specs = [tok, full(w), full(bias), full(lam), pl.BlockSpec((1, 1, c), lambda bi, i: (bi, 0, 0))]
    args = [xl, w, bias, lam, h0]
    out = jax.ShapeDtypeStruct((b, l, c), F32)
    if reverse:
        lg_chunk = (lambda bi, i: (bi, nc - 1 - i, 1))
        in_specs += [tok, pl.BlockSpec((1, tc, c), lg_chunk)]
        args += [hf, rest]
        out_shape, out_specs = (out, out), (tok, tok)
    else:
        out_shape, out_specs = out, tok
    return pl.pallas_call(
        functools.partial(_lru_kernel, reverse=reverse),
        out_shape=out_shape,
        grid=(b, nc),
        in_specs=in_specs,
        out_specs=out_specs,
        scratch_shapes=[pltpu.VMEM((tc, c), F32), pltpu.VMEM((tc, c), F32), pltpu.VMEM((SUBLANES, c), F32)],
        compiler_params=_cparams("arbitrary", "arbitrary"),
        name="lru_bwd" if reverse else "lru_fwd",
    )(*args)


def _lru_gate_weights(gw, gb):
    nb, bw = gw.shape[1], gw.shape[2]
    cols = []
    for j in range(2):
        m = jnp.zeros((nb * bw, nb * bw), F32)
        for n in range(nb):
            m = m.at[n * bw:(n + 1) * bw, n * bw:(n + 1) * bw].set(gw[j, n])
        cols.append(m)
    return jnp.concatenate(cols, axis=1).astype(BF16), gb.reshape(1, -1)


def _filter_kernel(w1_ref, b1_ref, w2_ref, b2_ref, w3_ref, fr_ref, dec_ref, h_ref, n_ref, *, seq):
    i = pl.program_id(0)
    tl = h_ref.shape[1]
    c = h_ref.shape[2]
    t = (i * tl + lax.broadcasted_iota(I32, (tl, LANES), 0)).astype(F32)
    lane = lax.broadcasted_iota(I32, (tl, LANES), 1)
    t01 = t * (1.0 / (seq - 1))
    kidx = jnp.where(lane <= HY_BANDS, lane - 1, lane - 1 - HY_BANDS).astype(F32)
    band = 1e-4 + kidx * ((HY_BANDS - 1 - 1e-4) / (HY_BANDS - 1))
    ang = (2.0 * math.pi / seq) * t * band
    z = jnp.where(lane == 0, t01,
                  jnp.where(lane <= HY_BANDS, jnp.cos(ang),
                            jnp.where(lane <= 2 * HY_BANDS, -jnp.sin(ang), 0.0)))
    fr = fr_ref[...]
    h = jnp.sin(fr * (_dot3(z, w1_ref[...]) + b1_ref[...]))
    h = jnp.sin(fr * (_dot3(h, w2_ref[...]) + b2_ref[...]))
    h = _dot3(h, w3_ref[...]) * jnp.exp(-t01[:, 0:1] * jnp.abs(dec_ref[...]))
    hf = h[:, 0:c]
    hb = jnp.where(t[:, 0:1] == 0.0, 0.0, h[:, c:])
    h_ref[0] = hf
    h_ref[1] = hb

    @pl.when(i == 0)
    def _():
        n_ref[...] = jnp.zeros(n_ref.shape, F32)

    n_ref[...] += jnp.sum(jnp.abs(hf) + jnp.abs(hb), axis=0, keepdims=True)


def _filter_call(w1p, b1, w2, b2, w3, fr, dec, seq, tl):
    c = w3.shape[1] // 2
    full = lambda a: pl.BlockSpec(a.shape, lambda i: (0,) * a.ndim)
    args = [w1p, b1, w2, b2, w3, fr, dec]
    return pl.pallas_call(
        functools.partial(_filter_kernel, seq=seq),
        out_shape=(jax.ShapeDtypeStruct((2, seq, c), F32), jax.ShapeDtypeStruct((1, c), F32)),
        grid=(seq // tl,),
        in_specs=[full(a) for a in args],
        out_specs=(pl.BlockSpec((2, tl, c), lambda i: (0, i, 0)), pl.BlockSpec((1, c), lambda i: (0, 0))),
        compiler_params=_cparams("arbitrary"),
        name="hyena_filter",
    )(*args)


def _cis_tables(na, nb, n):
    prod = (jnp.arange(na, dtype=I32)[:, None] * jnp.arange(nb, dtype=I32)[None, :]) % n
    ang = prod.astype(F32) * (2.0 * math.pi / n)
    return jnp.cos(ang), jnp.sin(ang)


def _lane_rep(x):
    return jnp.broadcast_to(x[:, :, None], x.shape + (LANES,))


def _dft_a_kernel(v_ref, f_ref, tc_ref, ts_ref, o_ref):
    n1 = tc_ref.shape[1]
    c = v_ref.shape[3]
    rep = c // LANES
    for j in range(v_ref.shape[1]):
        p_ = _dot(f_ref[...], v_ref[0, j].astype(BF16))
        pr, pi = p_[0:n1], p_[n1:]
        tc = jnp.concatenate([tc_ref[j]] * rep, axis=1)
        ts = jnp.concatenate([ts_ref[j]] * rep, axis=1)
        o_ref[0, j, 0] = (pr * tc + pi * ts).astype(BF16)
        o_ref[0, j, 1] = (pi * tc - pr * ts).astype(BF16)


def _dft_a_call(vp, f1s, tc, ts, cs2):
    nb, n2, half, c = vp.shape
    n1 = 2 * half
    full = lambda a: pl.BlockSpec(a.shape, lambda b, i: (0,) * a.ndim)
    return pl.pallas_call(
        _dft_a_kernel,
        out_shape=jax.ShapeDtypeStruct((nb, n2, 2, n1, c), BF16),
        grid=(nb, n2 // cs2),
        in_specs=[pl.BlockSpec((1, cs2, half, c), lambda b, i: (b, i, 0, 0)), full(f1s),
                  pl.BlockSpec((cs2, n1, LANES), lambda b, i: (i, 0, 0)),
                  pl.BlockSpec((cs2, n1, LANES), lambda b, i: (i, 0, 0))],
        out_specs=pl.BlockSpec((1, cs2, 2, n1, c), lambda b, i: (b, i, 0, 0, 0)),
        compiler_params=_cparams("arbitrary", "arbitrary"),
        name="dft_stage_a",
    )(vp, f1s, tc, ts)


def _spec_kernel(a_ref, mf_ref, inv_ref, o_ref):
    n2 = a_ref.shape[3]
    c = a_ref.shape[4]
    inv = inv_ref[...]
    for j in range(a_ref.shape[1]):
        yf = _dot(mf_ref[...], a_ref[0, j].reshape(2 * n2, c))
        yb = _dot(mf_ref[...], a_ref[1, j].reshape(2 * n2, c))
        o_ref[j, 0] = (yf[0:n2] + yb[0:n2]) * inv
        o_ref[j, 1] = (yf[n2:] - yb[n2:]) * inv


def _spec_call(at, mf, inv, cf):
    _, n1, _, n2, c = at.shape
    full = lambda a: pl.BlockSpec(a.shape, lambda i: (0,) * a.ndim)
    return pl.pallas_call(
        _spec_kernel,
        out_shape=jax.ShapeDtypeStruct((n1, 2, n2, c), F32),
        grid=(n1 // cf,),
        in_specs=[pl.BlockSpec((2, cf, 2, n2, c), lambda i: (0, i, 0, 0, 0)), full(mf), full(inv)],
        out_specs=pl.BlockSpec((cf, 2, n2, c), lambda i: (i, 0, 0, 0)),
        compiler_params=_cparams("arbitrary"),
        name="hyena_spectrum",
    )(at, mf, inv)


def _dft_b_kernel(a_ref, k_ref, mf_ref, mi_ref, tc_ref, ts_ref, o_ref):
    n2 = a_ref.shape[3]
    c = a_ref.shape[4]
    rep = c // LANES
    for j in range(a_ref.shape[1]):
        y = _dot(mf_ref[...], a_ref[0, j].reshape(2 * n2, c))
        yr, yi = y[0:n2], y[n2:]
        kr, ki = k_ref[j, 0], k_ref[j, 1]
        prod = jnp.concatenate([yr * kr - yi * ki, yr * ki + yi * kr], axis=0).astype(BF16)
        z = _dot(mi_ref[...], prod)
        zr, zi = z[0:n2], z[n2:]
        tc = jnp.concatenate([tc_ref[j]] * rep, axis=1)
        ts = jnp.concatenate([ts_ref[j]] * rep, axis=1)
        o_ref[0, j, 0] = (zr * tc - zi * ts).astype(BF16)
        o_ref[0, j, 1] = (zr * ts + zi * tc).astype(BF16)


def _dft_b_call(at, kspec, mf, mi, tc, ts, cf):
    nb, n1, _, n2, c = at.shape
    full = lambda a: pl.BlockSpec(a.shape, lambda b, i: (0,) * a.ndim)
    blk = pl.BlockSpec((1, cf, 2, n2, c), lambda b, i: (b, i, 0, 0, 0))
    tw = pl.BlockSpec((cf, n2, LANES), lambda b, i: (i, 0, 0))
    return pl.pallas_call(
        _dft_b_kernel,
        out_shape=jax.ShapeDtypeStruct((nb, n1, 2, n2, c), BF16),
        grid=(nb, n1 // cf),
        in_specs=[blk, pl.BlockSpec((cf, 2, n2, c), lambda b, i: (i, 0, 0, 0)), full(mf), full(mi), tw, tw],
        out_specs=blk,
        compiler_params=_cparams("arbitrary", "arbitrary"),
        name="dft_stage_b",
    )(at, kspec, mf, mi, tc, ts)


def _dft_c_kernel(z_ref, g_ref, vv_ref, x0_ref, bias_ref, o_ref):
    n1 = z_ref.shape[3]
    c = z_ref.shape[4]
    for j in range(z_ref.shape[1]):
        y = _dot(g_ref[...], z_ref[0, j].reshape(2 * n1, c))
        o_ref[0, j] = (y + vv_ref[0, j] * bias_ref[...]) * x0_ref[0, j]


def _dft_c_call(zt, g, vvp, x0p, bias, cs2):
    nb, n2, _, n1, c = zt.shape
    half = n1 // 2
    full = lambda a: pl.BlockSpec(a.shape, lambda b, i: (0,) * a.ndim)
    tok = pl.BlockSpec((1, cs2, half, c), lambda b, i: (b, i, 0, 0))
    return pl.pallas_call(
        _dft_c_kernel,
        out_shape=jax.ShapeDtypeStruct((nb, n2, half, c), F32),
        grid=(nb, n2 // cs2),
        in_specs=[pl.BlockSpec((1, cs2, 2, n1, c), lambda b, i: (b, i, 0, 0, 0)), full(g), tok, tok, full(bias)],
        out_specs=tok,
        compiler_params=_cparams("arbitrary", "arbitrary"),
        name="dft_stage_c",
    )(zt, g, vvp, x0p, bias)


def _swap_time(x, n2):
    b, l, c = x.shape
    return x.reshape(b, l // n2, n2, c).transpose(0, 2, 1, 3)


def _hyena_two_stage(vv, x0, hfb, inv, bias):
    b, l, c = vv.shape
    n2 = DFT_N2
    n = 2 * l
    n1 = n // n2
    half = n1 // 2
    cs2 = 8
    cf = 8
    c1, s1 = _cis_tables(n1, half, n1)
    f1s = jnp.concatenate([c1, -s1], axis=0).astype(BF16)
    c2, s2 = _cis_tables(n2, n2, n2)
    mf = jnp.concatenate([jnp.concatenate([c2, s2], 1), jnp.concatenate([-s2, c2], 1)], 0).astype(BF16)
    mi = jnp.concatenate([jnp.concatenate([c2, -s2], 1), jnp.concatenate([s2, c2], 1)], 0).astype(BF16)
    gc, gs = _cis_tables(half, n1, n1)
    g = (jnp.concatenate([gc, -gs], axis=1) * (1.0 / n)).astype(BF16)
    twa = [_lane_rep(t) for t in _cis_tables(n2, n1, n)]
    twb = [_lane_rep(t) for t in _cis_tables(n1, n2, n)]
    flip = lambda a: a.transpose(0, 3, 2, 1, 4)
    ka = _dft_a_call(_swap_time(hfb, n2), f1s, twa[0], twa[1], cs2)
    kspec = _spec_call(flip(ka), mf, inv, cf)
    vvp = _swap_time(vv, n2)
    a = _dft_a_call(vvp, f1s, twa[0], twa[1], cs2)
    z = _dft_b_call(flip(a), kspec, mf, mi, twb[0], twb[1], cf)
    yp = _dft_c_call(flip(z), g, vvp, _swap_time(x0, n2), bias, cs2)
    return yp.transpose(0, 2, 1, 3).reshape(b, l, c)


def _dense_conv_kernel(v_ref, x0_ref, h_ref, inv_ref, bias_ref, fd_ref, gi_ref, o_ref):
    n = fd_ref.shape[0] // 2
    fd = fd_ref[...]
    hf = _dot(fd, h_ref[0].astype(BF16))
    hb = _dot(fd, h_ref[1].astype(BF16))
    inv = inv_ref[...]
    kr = (hf[0:n] + hb[0:n]) * inv
    ki = (hf[n:] - hb[n:]) * inv
    v = v_ref[0]
    x = _dot(fd, v.astype(BF16))
    xr, xi = x[0:n], x[n:]
    prod = jnp.concatenate([xr * kr - xi * ki, xr * ki + xi * kr], axis=0).astype(BF16)
    y = _dot(gi_ref[...], prod)
    o_ref[0] = (y + v * bias_ref[...]) * x0_ref[0]


def _hyena_dense(vv, x0, hfb, inv, bias):
    b, l, c = vv.shape
    n = 2 * l
    cf, sf = _cis_tables(n, l, n)
    fd = jnp.concatenate([cf, -sf], axis=0).astype(BF16)
    ci, si = _cis_tables(l, n, n)
    gi = (jnp.concatenate([ci, -si], axis=1) * (1.0 / n)).astype(BF16)
    full = lambda a: pl.BlockSpec(a.shape, lambda bi: (0,) * a.ndim)
    tok = pl.BlockSpec((1, l, c), lambda bi: (bi, 0, 0))
    return pl.pallas_call(
        _dense_conv_kernel,
        out_shape=jax.ShapeDtypeStruct((b, l, c), F32),
        grid=(b,),
        in_specs=[tok, tok, full(hfb), full(inv), full(bias), full(fd), full(gi)],
        out_specs=tok,
        compiler_params=_cparams("arbitrary"),
        name="hyena_dense",
    )(vv, x0, hfb, inv, bias, fd, gi)


def _hyena(vv, x0, p):
    l = vv.shape[1]
    w1p = jnp.zeros((LANES, HY_FFN), F32).at[0:HY_EMB].set(p['hy_w1'])
    row = lambda a: a.reshape(1, -1)
    hfb, norm = _filter_call(w1p, row(p['hy_b1']), p['hy_w2'], row(p['hy_b2']), p['hy_w3'], row(p['hy_freq']),
                             row(p['hy_decay']), l, min(l, 512))
    inv = 1.0 / norm
    bias = row(p['hy_bias'])
    if l >= 8 * DFT_N2:
        return _hyena_two_stage(vv, x0, hfb, inv, bias)
    return _hyena_dense(vv, x0, hfb, inv, bias)


def _post_kernel(att_ref, lru_ref, hy_ref, x_ref, mod_ref, og_ref, wo_ref, fg_ref, rwh_ref, rwl_ref, rb_ref,
                 x1_ref, f_ref, sel_ref, gate_ref, *, ctx_row):
    row = pl.program_id(0) if ctx_row is None else ctx_row
    d = x_ref.shape[2]
    tm = x_ref.shape[1]
    a_w, l_w = ATT_WIDTH, LRU_WIDTH
    o = (_dot(_rms(att_ref[0], og_ref[:, 0:a_w]).astype(BF16), wo_ref[0:a_w, :])
         + _dot(_rms(lru_ref[0], og_ref[:, a_w:a_w + l_w]).astype(BF16), wo_ref[a_w:a_w + l_w, :])
         + _dot(_rms(hy_ref[0], og_ref[:, a_w + l_w:]).astype(BF16), wo_ref[a_w + l_w:, :]))
    mod = lambda k: mod_ref[pl.ds(row, 1), k * d:(k + 1) * d]
    x1 = x_ref[0] + mod(2) * o
    x1_ref[0] = x1
    f = _rms(x1, fg_ref[...]) * (1.0 + mod(4)) + mod(3)
    f_ref[0] = f
    fh, fl = _split(f)
    logits = _dot(fh, rwh_ref[...]) + (_dot(fl, rwh_ref[...]) + _dot(fh, rwl_ref[...])) + rb_ref[...]
    lane = lax.broadcasted_iota(I32, (tm, LANES), 1)
    sel = jnp.zeros((tm, LANES), I32)
    vals = jnp.zeros((tm, LANES), F32)
    for k in range(TOP_K):
        m = jnp.max(logits, axis=1, keepdims=True)
        idx = jnp.min(jnp.where(logits == m, lane, LANES), axis=1, keepdims=True)
        sel = jnp.where(lane == k, idx, sel)
        vals = jnp.where(lane == k, m, vals)
        logits = jnp.where(lane == idx, -jnp.inf, logits)
    e = jnp.where(lane < TOP_K, jnp.exp(vals - vals[:, 0:1]), 0.0)
    sel_ref[0] = sel
    gate_ref[0] = e / jnp.sum(e, axis=1, keepdims=True)


def _post_call(att, lru, hyo, x, mod_l, og, wo, fg, rwh, rwl, rb, ctx_row, tm):
    b, l, d = x.shape
    full = lambda a: pl.BlockSpec(a.shape, lambda bi, i: (0,) * a.ndim)
    tok = lambda w: pl.BlockSpec((1, tm, w), lambda bi, i: (bi, i, 0))
    consts = [mod_l, og, wo, fg, rwh, rwl, rb]
    return pl.pallas_call(
        functools.partial(_post_kernel, ctx_row=ctx_row),
        out_shape=(jax.ShapeDtypeStruct((b, l, d), F32), jax.ShapeDtypeStruct((b, l, d), F32),
                   jax.ShapeDtypeStruct((b, l, LANES), I32), jax.ShapeDtypeStruct((b, l, LANES), F32)),
        grid=(b, l // tm),
        in_specs=[tok(ATT_WIDTH), tok(LRU_WIDTH), tok(HY_WIDTH), tok(d)] + [full(a) for a in consts],
        out_specs=(tok(d), tok(d), tok(LANES), tok(LANES)),
        compiler_params=_cparams("arbitrary", "arbitrary"),
        name="post_lat" if ctx_row is None else "post_ctx",
    )(att, lru, hyo, x, *consts)


def _route(sel, tm, n_tiles):
    t = sel.shape[0]
    flat = sel.reshape(-1)
    order = jnp.argsort(flat, stable=True).astype(I32)
    counts = jnp.sum((flat[:, None] == jnp.arange(N_EXPERTS, dtype=I32)[None, :]).astype(I32), axis=0)
    tiles_per = (counts + tm - 1) // tm
    tile_end = jnp.cumsum(tiles_per)
    pad_start = (tile_end - tiles_per) * tm
    grp_start = jnp.cumsum(counts) - counts
    tile_ids = jnp.arange(n_tiles, dtype=I32)
    tile_expert = jnp.minimum(jnp.searchsorted(tile_end, tile_ids, side='right').astype(I32), N_EXPERTS - 1)
    rows = jnp.arange(n_tiles * tm, dtype=I32)
    row_e = tile_expert[rows // tm]
    rank = rows - pad_start[row_e]
    valid = (rank < counts[row_e]) & (rows // tm < tile_end[N_EXPERTS - 1])
    src = jnp.where(valid, order[jnp.clip(grp_start[row_e] + rank, 0, TOP_K * t - 1)] // TOP_K, 0)
    e_sorted = flat[order]
    dest = pad_start[e_sorted] + jnp.arange(TOP_K * t, dtype=I32) - grp_start[e_sorted]
    pos = dest[jnp.argsort(order).astype(I32)].reshape(t, TOP_K)
    n_used = tile_end[N_EXPERTS - 1:N_EXPERTS].astype(I32)
    return src.reshape(n_tiles, tm).astype(I32), pos.astype(I32), tile_expert, n_used


def _moe_kernel(te_ref, nu_ref, src_hbm, f_hbm, w1g_ref, w1l_ref, b1g_ref, b1l_ref, w2_ref, b2_ref, o_ref,
                idx_sm, xbuf, sem_idx, sem_row):
    i = pl.program_id(0)
    tm = xbuf.shape[1]
    n_used = nu_ref[0]
    slot = i % 2

    def idx_copy(tile, s):
        return pltpu.make_async_copy(src_hbm.at[tile], idx_sm.at[s], sem_idx.at[s])

    def start_rows(s):
        def issue(r, carry):
            pltpu.make_async_copy(f_hbm.at[pl.ds(idx_sm[s, r], 1), :], xbuf.at[s, pl.ds(r, 1), :],
                                  sem_row.at[s]).start()
            return carry
        lax.fori_loop(0, tm, issue, 0, unroll=8)

    @pl.when(i == 0)
    def _():
        idx_copy(0, 0).start()
        idx_copy(0, 0).wait()
        start_rows(0)

        @pl.when(n_used > 1)
        def _():
            idx_copy(1, 1).start()

    @pl.when(i + 1 < n_used)
    def _():
        idx_copy(i + 1, 1 - slot).wait()
        start_rows(1 - slot)

        @pl.when(i + 2 < n_used)
        def _():
            idx_copy(i + 2, slot).start()

    @pl.when(i < n_used)
    def _():
        pltpu.make_async_copy(f_hbm.at[pl.ds(0, tm), :], xbuf.at[slot], sem_row.at[slot]).wait()
        xb = xbuf[slot].astype(BF16)
        glu = jnp.minimum(_dot(xb, w1g_ref[0]) + b1g_ref[0], SWIGLU_LIMIT)
        lin = jnp.clip(_dot(xb, w1l_ref[0]) + b1l_ref[0], -SWIGLU_LIMIT, SWIGLU_LIMIT)
        act = glu * _sigmoid(SWIGLU_ALPHA * glu) * (lin + 1.0)
        o_ref[...] = _dot(act.astype(BF16), w2_ref[0]) + b2_ref[0]

    @pl.when(i >= n_used)
    def _():
        o_ref[...] = jnp.zeros(o_ref.shape, F32)


def _moe_call(tile_expert, n_used, src, f, w1g, w1l, b1g, b1l, w2, b2, tm):
    n_tiles = src.shape[0]
    d = f.shape[1]
    dff = w1g.shape[2]
    wspec = lambda shape: pl.BlockSpec((1,) + shape, lambda i, te, nu: (te[i], 0, 0))
    grid_spec = pltpu.PrefetchScalarGridSpec(
        num_scalar_prefetch=2,
        grid=(n_tiles,),
        in_specs=[pl.BlockSpec(memory_space=pl.ANY), pl.BlockSpec(memory_space=pl.ANY),
                  wspec((d, dff)), wspec((d, dff)), wspec((1, dff)), wspec((1, dff)), wspec((dff, d)), wspec((1, d))],
        out_specs=pl.BlockSpec((tm, d), lambda i, te, nu: (i, 0)),
        scratch_shapes=[pltpu.SMEM((2, tm), I32), pltpu.VMEM((2, tm, d), F32),
                        pltpu.SemaphoreType.DMA((2,)), pltpu.SemaphoreType.DMA((2,))],
    )
    return pl.pallas_call(
        _moe_kernel,
        out_shape=jax.ShapeDtypeStruct((n_tiles * tm, d), F32),
        grid_spec=grid_spec,
        compiler_params=_cparams("arbitrary"),
        name="moe_experts",
    )(tile_expert, n_used, src, f, w1g, w1l, b1g, b1l, w2, b2)


def _combine_kernel(pos_hbm, y_hbm, gate_ref, x1_ref, mod_ref, o_ref, idx_sm, ybuf, sem_idx, sem_row,
                    *, lat_tiles, tiles_per_batch, ctx_row):
    i = pl.program_id(0)
    n = pl.num_programs(0)
    tc = x1_ref.shape[0]
    d = x1_ref.shape[1]
    slot = i % 2

    def idx_copy(tile, s):
        return pltpu.make_async_copy(pos_hbm.at[tile], idx_sm.at[s], sem_idx.at[s])

    def start_rows(s):
        def issue(r, carry):
            pltpu.make_async_copy(y_hbm.at[pl.ds(idx_sm[s, r], 1), :], ybuf.at[s, pl.ds(r, 1), :],
                                  sem_row.at[s]).start()
            return carry
        lax.fori_loop(0, TOP_K * tc, issue, 0, unroll=8)

    @pl.when(i == 0)
    def _():
        idx_copy(0, 0).start()
        idx_copy(0, 0).wait()
        start_rows(0)

        @pl.when(n > 1)
        def _():
            idx_copy(1, 1).start()

    @pl.when(i + 1 < n)
    def _():
        idx_copy(i + 1, 1 - slot).wait()
        start_rows(1 - slot)

        @pl.when(i + 2 < n)
        def _():
            idx_copy(i + 2, slot).start()

    pltpu.make_async_copy(y_hbm.at[pl.ds(0, TOP_K * tc), :], ybuf.at[slot], sem_row.at[slot]).wait()
    row = jnp.where(i < lat_tiles, i // tiles_per_batch, ctx_row)
    gf = mod_ref[pl.ds(row, 1), 5 * d:6 * d]
    acc = None
    for j in range(TOP_K):
        term = gate_ref[:, j:j + 1] * ybuf[slot, j * tc:(j + 1) * tc, :]
        acc = term if acc is None else acc + term
    o_ref[...] = x1_ref[...] + gf * acc


def _combine_call(pos_tiles, y, gate, x1, mod_l, lat_tiles, tiles_per_batch, ctx_row, tc):
    t, d = x1.shape
    return pl.pallas_call(
        functools.partial(_combine_kernel, lat_tiles=lat_tiles, tiles_per_batch=tiles_per_batch, ctx_row=ctx_row),
        out_shape=jax.ShapeDtypeStruct((t, d), F32),
        grid=(t // tc,),
        in_specs=[pl.BlockSpec(memory_space=pl.ANY), pl.BlockSpec(memory_space=pl.ANY),
                  pl.BlockSpec((tc, LANES), lambda i: (i, 0)), pl.BlockSpec((tc, d), lambda i: (i, 0)),
                  pl.BlockSpec(mod_l.shape, lambda i: (0, 0))],
        out_specs=pl.BlockSpec((tc, d), lambda i: (i, 0)),
        scratch_shapes=[pltpu.SMEM((2, TOP_K * tc), I32), pltpu.VMEM((2, TOP_K * tc, d), F32),
                        pltpu.SemaphoreType.DMA((2,)), pltpu.SemaphoreType.DMA((2,))],
        compiler_params=_cparams("arbitrary"),
        name="moe_combine",
    )(pos_tiles, y, gate, x1, mod_l)


def _moe(f, sel, gate, x1, mod_l, ew, lat_tiles_rows, seq, ctx_row):
    t, d = f.shape
    tm = 256
    tc = 128
    n_tiles = TOP_K * t // tm + N_EXPERTS
    src, pos, tile_expert, n_used = _route(sel[:, 0:TOP_K], tm, n_tiles)
    y = _moe_call(tile_expert, n_used, src, f, *ew, tm)
    pos_tiles = pos.reshape(t // tc, tc, TOP_K).transpose(0, 2, 1).reshape(t // tc, TOP_K * tc)
    return _combine_call(pos_tiles, y, gate, x1, mod_l, lat_tiles_rows // tc, seq // tc, ctx_row, tc)


def _rope_tables(s):
    rows = (jnp.arange(s, dtype=I32) // GRID_W).astype(F32)
    cols = (jnp.arange(s, dtype=I32) % GRID_W).astype(F32)
    nfreq = HEAD_DIM // 4
    inv = ROPE_THETA ** (-jnp.arange(nfreq, dtype=F32) / nfreq)
    ang = jnp.concatenate([rows[:, None] * inv, cols[:, None] * inv], axis=-1)
    cos, sin = jnp.cos(ang), jnp.sin(ang)
    reps = LANES // HEAD_DIM
    return jnp.tile(jnp.concatenate([cos, cos], -1), (1, reps)), jnp.tile(jnp.concatenate([-sin, sin], -1), (1, reps))


def _pick_tk(lk):
    for k in (768, 512, 256, 128):
        if lk % k == 0:
            return k
    return lk


def kernel(x, c, ctx, c_ctx, w_mod, b_mod, norm_mix_g, norm_ffn_g, w_in, q_norm_g, k_norm_g, lru_conv_w, lru_conv_b, lru_gate_w, lru_gate_b, lru_lambda, hy_conv_w, hy_conv_b, hy_w1, hy_b1, hy_w2, hy_b2, hy_w3, hy_freq, hy_decay, hy_bias, out_norm_g, w_out, router_w, router_b, moe_w1, moe_b1, moe_w2, moe_b2):
    b, s, d = x.shape
    lc = ctx.shape[1]
    depth = w_mod.shape[0]
    ctx_row = b
    assert b < SUBLANES and d == D_MODEL
    c8 = jnp.concatenate([c, c_ctx[None, :], jnp.zeros((SUBLANES - b - 1, d), F32)], axis=0)
    mods = _mod_call(c8, w_mod, b_mod)
    rope_tabs = _rope_tables(s)
    lane = jnp.arange(LANES)
    bd = ((lane[:, None] // HEAD_DIM) == (lane[None, :] // HEAD_DIM)).astype(F32) * (1.0 / HEAD_DIM)
    bd = bd.astype(BF16)
    row = lambda a: a.reshape(1, -1)
    tile_heads = lambda g: jnp.tile(g.reshape(1, HEAD_DIM), (1, LANES // HEAD_DIM))
    tm = min(512, s)
    tmc = min(256, lc)
    t_lat = b * s

    for l in range(depth):
        last = l == depth - 1
        mod_l = mods[l]
        w_in_l = w_in[l].astype(BF16)
        qg, kg = tile_heads(q_norm_g[l]), tile_heads(k_norm_g[l])
        p = {'hy_w1': hy_w1[l], 'hy_b1': hy_b1[l], 'hy_w2': hy_w2[l], 'hy_b2': hy_b2[l], 'hy_w3': hy_w3[l],
             'hy_freq': hy_freq[l], 'hy_decay': hy_decay[l], 'hy_bias': hy_bias[l]}
        conv = (lru_conv_w[l], row(lru_conv_b[l]), hy_conv_w[l], row(hy_conv_b[l]))

        q, k, v, rest = _in_proj_call(x, mod_l, row(norm_mix_g[l]), w_in_l, qg, kg, bd, rope_tabs, None, tm)
        qc, kc, vc, rest_c = _in_proj_call(ctx, mod_l, row(norm_mix_g[l]), w_in_l, qg, kg, bd, None, ctx_row, tmc)
        xl, vv, x0 = _prep_call(rest, *conv, tm)
        xl_c, vv_c, x0_c = _prep_call(rest_c, *conv, tmc)

        kd = _dup_heads(jnp.concatenate([kc, k], axis=1))
        vd = _dup_heads(jnp.concatenate([vc, v], axis=1))
        att = _attn_call(q, kd, vd, min(256, s), _pick_tk(lc + s))

        zeros_h = jnp.zeros((b, 1, LRU_WIDTH), F32)
        gws = [_lru_gate_weights(lru_gate_w[l, dr], lru_gate_b[l, dr]) for dr in range(2)]
        lams = [row(lru_lambda[l, dr]) for dr in range(2)]
        hc_f = _lru_call(xl_c, *gws[0], lams[0], zeros_h, False, tmc)
        hc_b, lru_c = _lru_call(xl_c, *gws[1], lams[1], zeros_h, True, tmc, hf=hc_f, rest=rest_c)
        h_f = _lru_call(xl, *gws[0], lams[0], hc_f[:, lc - 1:lc], False, tm)
        _, lru = _lru_call(xl, *gws[1], lams[1], hc_b[:, 0:1], True, tm, hf=h_f, rest=rest)

        hyo = _hyena(vv, x0, p)

        rw = jnp.zeros((d, LANES), F32).at[:, 0:N_EXPERTS].set(router_w[l])
        rwh, rwl = _split(rw)
        rb = jnp.full((1, LANES), -1e30, F32).at[0, 0:N_EXPERTS].set(router_b[l])
        post_consts = (row(out_norm_g[l]), w_out[l].astype(BF16), row(norm_ffn_g[l]), rwh, rwl, rb)
        x1, f, sel, gate = _post_call(att, lru, hyo, x, mod_l, *post_consts, None, tm)
        ew = (moe_w1[l, :, :, 0::2].astype(BF16), moe_w1[l, :, :, 1::2].astype(BF16),
              moe_b1[l, :, None, 0::2], moe_b1[l, :, None, 1::2], moe_w2[l].astype(BF16), moe_b2[l, :, None, :])
        flat = lambda a: a.reshape(-1, a.shape[-1])
        if last:
            x = _moe(flat(f), flat(sel), flat(gate), flat(x1), mod_l, ew, t_lat, s, ctx_row).reshape(b, s, d)
        else:
            att_c = _attn_call(qc, _dup_heads(kc), _dup_heads(vc), tmc, _pick_tk(lc))
            hyo_c = _hyena(vv_c, x0_c, p)
            x1c, fc, selc, gatec = _post_call(att_c, lru_c, hyo_c, ctx, mod_l, *post_consts, ctx_row, tmc)
            cat = lambda a, bb: jnp.concatenate([flat(a), flat(bb)], axis=0)
            out = _moe(cat(f, fc), cat(sel, selc), cat(gate, gatec), cat(x1, x1c), mod_l, ew, t_lat, s, ctx_row)
            x = out[0:t_lat].reshape(b, s, d)
            ctx = out[t_lat:].reshape(b, lc, d)
    return x
```

```python
import functools
import math

import jax
import jax.numpy as jnp
from jax import lax
from jax.experimental import pallas as pl
from jax.experimental.pallas import tpu as pltpu

F32 = jnp.float32
BF16 = jnp.bfloat16
I32 = jnp.int32

D_MODEL = 1024
HEAD_DIM = 64
ATT_WIDTH = 512
KV_WIDTH = 128
LRU_WIDTH = 256
HY_WIDTH = 256
REST_WIDTH = 2 * LRU_WIDTH + 3 * HY_WIDTH
IN_WIDTH = ATT_WIDTH + 2 * KV_WIDTH + REST_WIDTH
GRID_W = 64
ROPE_THETA = 10000.0
LRU_C = 8.0
HY_EMB = 33
HY_BANDS = 16
HY_FFN = 64
N_EXPERTS = 32
TOP_K = 4
D_FF = 1024
SWIGLU_ALPHA = 1.702
SWIGLU_LIMIT = 7.0
EPS = 1e-6
LANES = 128
SUBLANES = 8
DFT_N2 = 128
VMEM_LIMIT = 56 * 1024 * 1024
MOE_TILE = 256
IDX_STEP = 1024
IDX_WINDOW = 2 * IDX_STEP
Q_SCALE = HEAD_DIM ** -0.5 * math.log2(math.e)


def _cparams(*sem):
    return pltpu.CompilerParams(dimension_semantics=sem, vmem_limit_bytes=VMEM_LIMIT)


def _dot(a, b):
    return jnp.dot(a, b, preferred_element_type=F32)


def _split(x):
    hi = x.astype(BF16)
    return hi, (x - hi.astype(F32)).astype(BF16)


def _dot3(a, b):
    ah, al = _split(a)
    bh, bl = _split(b)
    return _dot(ah, bh) + (_dot(al, bh) + _dot(ah, bl))


def _sigmoid(x):
    return 1.0 / (1.0 + jnp.exp(-x))


def _rms(x, g):
    return x * lax.rsqrt(jnp.mean(x * x, axis=-1, keepdims=True) + EPS) * g


def _mod_kernel(c_ref, w_ref, b_ref, o_ref):
    c = c_ref[...]
    o_ref[0] = _dot3(c * _sigmoid(c), w_ref[0]) + b_ref[0]


def _mod_call(c8, w_mod, b_mod):
    depth, d, n6 = w_mod.shape
    tn = n6 // 4
    return pl.pallas_call(
        _mod_kernel,
        out_shape=jax.ShapeDtypeStruct((depth, SUBLANES, n6), F32),
        grid=(depth, n6 // tn),
        in_specs=[pl.BlockSpec((SUBLANES, d), lambda l, j: (0, 0)),
                  pl.BlockSpec((1, d, tn), lambda l, j: (l, 0, j)),
                  pl.BlockSpec((1, 1, tn), lambda l, j: (l, 0, j))],
        out_specs=pl.BlockSpec((1, SUBLANES, tn), lambda l, j: (l, 0, j)),
        compiler_params=_cparams("arbitrary", "arbitrary"),
        name="mod",
    )(c8, w_mod, b_mod.reshape(depth, 1, n6))


def _head_norm(x, g, bd):
    hi, lo = _split(x * x)
    ms = _dot(hi, bd) + _dot(lo, bd)
    return x * lax.rsqrt(ms + EPS) * g


def _rope(y, cos, sin, first_half):
    swapped = jnp.where(first_half, pltpu.roll(y, LANES - HEAD_DIM // 2, 1), pltpu.roll(y, HEAD_DIM // 2, 1))
    return y * cos + swapped * sin


def _in_proj_kernel(*refs, rope, ctx_row):
    if rope:
        x_ref, mod_ref, g_ref, w_ref, qg_ref, kg_ref, bd_ref, cos_ref, sin_ref, q_ref, k_ref, v_ref, r_ref = refs
    else:
        x_ref, mod_ref, g_ref, w_ref, qg_ref, kg_ref, bd_ref, q_ref, k_ref, v_ref, r_ref = refs
    row = pl.program_id(0) if ctx_row is None else ctx_row
    d = x_ref.shape[2]
    tm = x_ref.shape[1]
    shift = mod_ref[pl.ds(row, 1), 0:d]
    scale = mod_ref[pl.ds(row, 1), d:2 * d]
    h = _rms(x_ref[0], g_ref[...]) * (1.0 + scale) + shift
    u = _dot(h.astype(BF16), w_ref[...])
    bd = bd_ref[...]
    if rope:
        cos = cos_ref[...]
        sin = sin_ref[...]
        lane = lax.broadcasted_iota(I32, (tm, LANES), 1)
        first_half = (lane & (HEAD_DIM // 2)) == 0
    for p in range(ATT_WIDTH // LANES):
        y = _head_norm(u[:, p * LANES:(p + 1) * LANES], qg_ref[...], bd)
        if rope:
            y = _rope(y, cos, sin, first_half)
        q_ref[0, :, p * LANES:(p + 1) * LANES] = (y * Q_SCALE).astype(BF16)
    y = _head_norm(u[:, ATT_WIDTH:ATT_WIDTH + KV_WIDTH], kg_ref[...], bd)
    if rope:
        y = _rope(y, cos, sin, first_half)
    k_ref[0] = y.astype(BF16)
    v_ref[0] = u[:, ATT_WIDTH + KV_WIDTH:ATT_WIDTH + 2 * KV_WIDTH].astype(BF16)
    r_ref[0] = u[:, ATT_WIDTH + 2 * KV_WIDTH:]


def _in_proj_call(x, mod_l, g, w_in, qg, kg, bd, rope_tabs, ctx_row, tm):
    b, l, d = x.shape
    rope = rope_tabs is not None
    full = lambda shape: pl.BlockSpec(shape, lambda bi, i: (0,) * len(shape))
    in_specs = [pl.BlockSpec((1, tm, d), lambda bi, i: (bi, i, 0)),
                full(mod_l.shape), full((1, d)), full(w_in.shape), full((1, LANES)), full((1, LANES)),
                full((LANES, LANES))]
    args = [x, mod_l, g, w_in, qg, kg, bd]
    if rope:
        in_specs += [pl.BlockSpec((tm, LANES), lambda bi, i: (i, 0))] * 2
        args += list(rope_tabs)
    tok = lambda w: pl.BlockSpec((1, tm, w), lambda bi, i: (bi, i, 0))
    return pl.pallas_call(
        functools.partial(_in_proj_kernel, rope=rope, ctx_row=ctx_row),
        out_shape=(jax.ShapeDtypeStruct((b, l, ATT_WIDTH), BF16), jax.ShapeDtypeStruct((b, l, KV_WIDTH), BF16),
                   jax.ShapeDtypeStruct((b, l, KV_WIDTH), BF16), jax.ShapeDtypeStruct((b, l, REST_WIDTH), F32)),
        grid=(b, l // tm),
        in_specs=in_specs,
        out_specs=(tok(ATT_WIDTH), tok(KV_WIDTH), tok(KV_WIDTH), tok(REST_WIDTH)),
        compiler_params=_cparams("arbitrary", "arbitrary"),
        name="in_proj_lat" if rope else "in_proj_ctx",
    )(*args)


def _conv_taps(win, w, left, tm):
    acc = None
    for j in range(w.shape[0]):
        off = SUBLANES + j - left
        term = win[off:off + tm] * w[j:j + 1]
        acc = term if acc is None else acc + term
    return acc


def _prep_kernel(cur_ref, prev_ref, next_ref, lw_ref, lb_ref, hw_ref, hb_ref, xl_ref, vv_ref, x0_ref):
    i = pl.program_id(1)
    tm = cur_ref.shape[1]
    has_prev = (i > 0).astype(F32)
    has_next = (i < pl.num_programs(1) - 1).astype(F32)

    def window(lo, hi):
        return jnp.concatenate([prev_ref[0, :, lo:hi] * has_prev, cur_ref[0, :, lo:hi],
                                next_ref[0, :, lo:hi] * has_next], axis=0)

    xl_ref[0] = _conv_taps(window(0, LRU_WIDTH), lw_ref[...], 2, tm) + lb_ref[...]
    uc = _conv_taps(window(2 * LRU_WIDTH, REST_WIDTH), hw_ref[...], 1, tm) + hb_ref[...]
    x0_ref[0] = uc[:, 0:HY_WIDTH]
    vv_ref[0] = uc[:, 2 * HY_WIDTH:] * uc[:, HY_WIDTH:2 * HY_WIDTH]


def _prep_call(rest, lw, lb, hw, hb, tm):
    b, l, w = rest.shape
    nb = tm // SUBLANES
    last = l // SUBLANES - 1
    full = lambda a: pl.BlockSpec(a.shape, lambda bi, i: (0,) * a.ndim)
    tok = pl.BlockSpec((1, tm, LRU_WIDTH), lambda bi, i: (bi, i, 0))
    out = jax.ShapeDtypeStruct((b, l, LRU_WIDTH), F32)
    return pl.pallas_call(
        _prep_kernel,
        out_shape=(out, out, out),
        grid=(b, l // tm),
        in_specs=[pl.BlockSpec((1, tm, w), lambda bi, i: (bi, i, 0)),
                  pl.BlockSpec((1, SUBLANES, w), lambda bi, i: (bi, jnp.maximum(i * nb - 1, 0), 0)),
                  pl.BlockSpec((1, SUBLANES, w), lambda bi, i: (bi, jnp.minimum((i + 1) * nb, last), 0)),
                  full(lw), full(lb), full(hw), full(hb)],
        out_specs=(tok, tok, tok),
        compiler_params=_cparams("arbitrary", "arbitrary"),
        name="conv_prep",
    )(rest, rest, rest, lw, lb, hw, hb)


def _attn_kernel(q_ref, k_ref, v_ref, o_ref, qst_sc, m_sc, l_sc, acc_sc, *, tk):
    tq = q_ref.shape[1]
    lk = k_ref.shape[2]
    nl = tk // LANES
    low = lax.broadcasted_iota(I32, (tq, LANES), 1) < HEAD_DIM
    for g in range(2):
        for n, p in enumerate((2 * g, 2 * g + 1)):
            qs = q_ref[0, :, p * LANES:(p + 1) * LANES]
            zero = jnp.zeros_like(qs)
            qst_sc[g, 2 * n * tq:(2 * n + 1) * tq, :] = jnp.where(low, qs, zero)
            qst_sc[g, (2 * n + 1) * tq:(2 * n + 2) * tq, :] = jnp.where(low, zero, qs)
    m_sc[...] = jnp.full(m_sc.shape, -jnp.inf, F32)
    l_sc[...] = jnp.zeros(l_sc.shape, F32)
    acc_sc[...] = jnp.zeros(acc_sc.shape, F32)

    def body(j, carry):
        start = pl.multiple_of(j * tk, tk)
        for g in range(2):
            kc = k_ref[0, g, pl.ds(start, tk), :]
            vc = v_ref[0, g, pl.ds(start, tk), :]
            s = lax.dot_general(qst_sc[g], kc, (((1,), (1,)), ((), ())), preferred_element_type=F32)
            cols = [s[:, c * LANES:(c + 1) * LANES] for c in range(nl)]
            mx = cols[0]
            for col in cols[1:]:
                mx = jnp.maximum(mx, col)
            m_old = m_sc[g]
            m_new = jnp.maximum(m_old, jnp.max(mx, axis=1, keepdims=True))
            alpha = jnp.exp2(m_old - m_new)
            ps = [jnp.exp2(col - m_new) for col in cols]
            lsum = ps[0]
            for p_ in ps[1:]:
                lsum = lsum + p_
            l_sc[g] = alpha * l_sc[g] + lsum
            pb = jnp.concatenate([p_.astype(BF16) for p_ in ps], axis=1)
            acc_sc[g] = alpha * acc_sc[g] + _dot(pb, vc)
            m_sc[g] = m_new
        return carry

    lax.fori_loop(0, lk // tk, body, 0)
    for g in range(2):
        o = acc_sc[g] / jnp.sum(l_sc[g], axis=1, keepdims=True)
        for n, p in enumerate((2 * g, 2 * g + 1)):
            o_ref[0, :, p * LANES:(p + 1) * LANES] = jnp.where(
                low, o[2 * n * tq:(2 * n + 1) * tq], o[(2 * n + 1) * tq:(2 * n + 2) * tq])


def _attn_call(q, kd, vd, tq, tk):
    b, l, _ = q.shape
    lk = kd.shape[2]
    kv = pl.BlockSpec((1, 2, lk, LANES), lambda bi, i: (bi, 0, 0, 0))
    return pl.pallas_call(
        functools.partial(_attn_kernel, tk=tk),
        out_shape=jax.ShapeDtypeStruct((b, l, ATT_WIDTH), F32),
        grid=(b, l // tq),
        in_specs=[pl.BlockSpec((1, tq, ATT_WIDTH), lambda bi, i: (bi, i, 0)), kv, kv],
        out_specs=pl.BlockSpec((1, tq, ATT_WIDTH), lambda bi, i: (bi, i, 0)),
        scratch_shapes=[pltpu.VMEM((2, 4 * tq, LANES), BF16), pltpu.VMEM((2, 4 * tq, LANES), F32),
                        pltpu.VMEM((2, 4 * tq, LANES), F32), pltpu.VMEM((2, 4 * tq, LANES), F32)],
        compiler_params=_cparams("arbitrary", "arbitrary"),
        name="attn",
    )(q, kd, vd)


def _dup_heads(x):
    halves = [jnp.concatenate([x[..., g * HEAD_DIM:(g + 1) * HEAD_DIM]] * 2, axis=-1) for g in range(2)]
    return jnp.stack(halves, axis=1)


def _gelu_tanh(x):
    return 0.5 * x * (1.0 + jnp.tanh(math.sqrt(2.0 / math.pi) * (x + 0.044715 * (x * x * x))))


def _lru_kernel(*refs, reverse):
    if reverse:
        xl_ref, w_ref, b_ref, lam_ref, h0_ref, hf_ref, lg_ref, h_ref, o_ref, a_sc, b_sc, c_sc = refs
    else:
        xl_ref, w_ref, b_ref, lam_ref, h0_ref, h_ref, a_sc, b_sc, c_sc = refs
    tc = xl_ref.shape[1]
    nt = tc // SUBLANES

    @pl.when(pl.program_id(1) == 0)
    def _():
        c_sc[...] = jnp.broadcast_to(h0_ref[0], c_sc.shape)

    xc = xl_ref[0]
    gates = _dot(xc.astype(BF16), w_ref[...]) + b_ref[...]
    r = _sigmoid(gates[:, 0:LRU_WIDTH])
    gi = _sigmoid(gates[:, LRU_WIDTH:])
    nlam = -lam_ref[...]
    softplus = jnp.maximum(nlam, 0.0) + jnp.log1p(jnp.exp(-jnp.abs(nlam)))
    log_a = -LRU_C * r * softplus
    a = jnp.exp(log_a)
    a_sc[...] = a
    b_sc[...] = jnp.sqrt(1.0 - a * a) * (gi * xc)
    row = lax.broadcasted_iota(I32, (SUBLANES, LRU_WIDTH), 0)

    def tile(k, carry):
        kk = nt - 1 - k if reverse else k
        start = pl.multiple_of(kk * SUBLANES, SUBLANES)
        a = a_sc[pl.ds(start, SUBLANES), :]
        bv = b_sc[pl.ds(start, SUBLANES), :]
        for dist in (1, 2, 4):
            shift = SUBLANES - dist if reverse else dist
            keep = (row < SUBLANES - dist) if reverse else (row >= dist)
            a_n = pltpu.roll(a, shift, 0)
            b_n = pltpu.roll(bv, shift, 0)
            bv = jnp.where(keep, a * b_n + bv, bv)
            a = jnp.where(keep, a * a_n, a)
        h = a * c_sc[...] + bv
        h_ref[0, pl.ds(start, SUBLANES), :] = h
        edge = h[0:1] if reverse else h[SUBLANES - 1:SUBLANES]
        c_sc[...] = jnp.broadcast_to(edge, c_sc.shape)
        return carry

    lax.fori_loop(0, nt, tile, 0)
    if reverse:
        o_ref[0] = (hf_ref[0] + h_ref[0]) * _gelu_tanh(lg_ref[0])


def _lru_call(xl, w, bias, lam, h0, reverse, tc, hf=None, rest=None):
    b, l, c = xl.shape
    nc = l // tc
    chunk = (lambda bi, i: (bi, nc - 1 - i, 0)) if reverse else (lambda bi, i: (bi, i, 0))
    full = lambda a: pl.BlockSpec(a.shape, lambda bi, i: (0,) * a.ndim)
    tok = pl.BlockSpec((1, tc, c), chunk)
    in_specs = [tok, full(w), full(bias), full(lam), pl.BlockSpec((1, 1, c), lambda bi, i: (bi, 0, 0))]
    args = [xl, w, bias, lam, h0]
    out = jax.ShapeDtypeStruct((b, l, c), F32)
    if reverse:
        lg_chunk = (lambda bi, i: (bi, nc - 1 - i, 1))
        in_specs += [tok, pl.BlockSpec((1, tc, c), lg_chunk)]
        args += [hf, rest]
        out_shape, out_specs = (out, out), (tok, tok)
    else:
        out_shape, out_specs = out, tok
    return pl.pallas_call(
        functools.partial(_lru_kernel, reverse=reverse),
        out_shape=out_shape,
        grid=(b, nc),
        in_specs=in_specs,
        out_specs=out_specs,
        scratch_shapes=[pltpu.VMEM((tc, c), F32), pltpu.VMEM((tc, c), F32), pltpu.VMEM((SUBLANES, c), F32)],
        compiler_params=_cparams("arbitrary", "arbitrary"),
        name="lru_bwd" if reverse else "lru_fwd",
    )(*args)


def _lru_gate_weights(gw, gb):
    nb, bw = gw.shape[1], gw.shape[2]
    cols = []
    for j in range(2):
        m = jnp.zeros((nb * bw, nb * bw), F32)
        for n in range(nb):
            m = m.at[n * bw:(n + 1) * bw, n * bw:(n + 1) * bw].set(gw[j, n])
        cols.append(m)
    return jnp.concatenate(cols, axis=1).astype(BF16), gb.reshape(1, -1)


def _filter_kernel(w1_ref, b1_ref, w2_ref, b2_ref, w3_ref, fr_ref, dec_ref, h_ref, n_ref, *, seq):
    i = pl.program_id(0)
    tl = h_ref.shape[1]
    c = h_ref.shape[2]
    t = (i * tl + lax.broadcasted_iota(I32, (tl, LANES), 0)).astype(F32)
    lane = lax.broadcasted_iota(I32, (tl, LANES), 1)
    t01 = t * (1.0 / (seq - 1))
    kidx = jnp.where(lane <= HY_BANDS, lane - 1, lane - 1 - HY_BANDS).astype(F32)
    band = 1e-4 + kidx * ((HY_BANDS - 1 - 1e-4) / (HY_BANDS - 1))
    ang = (2.0 * math.pi / seq) * t * band
    z = jnp.where(lane == 0, t01,
                  jnp.where(lane <= HY_BANDS, jnp.cos(ang),
                            jnp.where(lane <= 2 * HY_BANDS, -jnp.sin(ang), 0.0)))
    fr = fr_ref[...]
    h = jnp.sin(fr * (_dot3(z, w1_ref[...]) + b1_ref[...]))
    h = jnp.sin(fr * (_dot3(h, w2_ref[...]) + b2_ref[...]))
    h = _dot3(h, w3_ref[...]) * jnp.exp(-t01[:, 0:1] * jnp.abs(dec_ref[...]))
    hf = h[:, 0:c]
    hb = jnp.where(t[:, 0:1] == 0.0, 0.0, h[:, c:])
    h_ref[0] = hf
    h_ref[1] = hb

    @pl.when(i == 0)
    def _():
        n_ref[...] = jnp.zeros(n_ref.shape, F32)

    n_ref[...] += jnp.sum(jnp.abs(hf) + jnp.abs(hb), axis=0, keepdims=True)


def _filter_call(w1p, b1, w2, b2, w3, fr, dec, seq, tl):
    c = w3.shape[1] // 2
    full = lambda a: pl.BlockSpec(a.shape, lambda i: (0,) * a.ndim)
    args = [w1p, b1, w2, b2, w3, fr, dec]
    return pl.pallas_call(
        functools.partial(_filter_kernel, seq=seq),
        out_shape=(jax.ShapeDtypeStruct((2, seq, c), F32), jax.ShapeDtypeStruct((1, c), F32)),
        grid=(seq // tl,),
        in_specs=[full(a) for a in args],
        out_specs=(pl.BlockSpec((2, tl, c), lambda i: (0, i, 0)), pl.BlockSpec((1, c), lambda i: (0, 0))),
        compiler_params=_cparams("arbitrary"),
        name="hyena_filter",
    )(*args)


def _cis_tables(na, nb, n):
    prod = (jnp.arange(na, dtype=I32)[:, None] * jnp.arange(nb, dtype=I32)[None, :]) % n
    ang = prod.astype(F32) * (2.0 * math.pi / n)
    return jnp.cos(ang), jnp.sin(ang)


def _lane_rep(x):
    return jnp.broadcast_to(x[:, :, None], x.shape + (LANES,))


def _dft_a_kernel(v_ref, f_ref, tc_ref, ts_ref, o_ref):
    n1 = tc_ref.shape[1]
    c = v_ref.shape[3]
    rep = c // LANES
    for j in range(v_ref.shape[1]):
        p_ = _dot(f_ref[...], v_ref[0, j].astype(BF16))
        pr, pi = p_[0:n1], p_[n1:]
        tc = jnp.concatenate([tc_ref[j]] * rep, axis=1)
        ts = jnp.concatenate([ts_ref[j]] * rep, axis=1)
        o_ref[0, j, 0] = (pr * tc + pi * ts).astype(BF16)
        o_ref[0, j, 1] = (pi * tc - pr * ts).astype(BF16)


def _dft_a_call(vp, f1s, tc, ts, cs2):
    nb, n2, half, c = vp.shape
    n1 = 2 * half
    full = lambda a: pl.BlockSpec(a.shape, lambda b, i: (0,) * a.ndim)
    return pl.pallas_call(
        _dft_a_kernel,
        out_shape=jax.ShapeDtypeStruct((nb, n2, 2, n1, c), BF16),
        grid=(nb, n2 // cs2),
        in_specs=[pl.BlockSpec((1, cs2, half, c), lambda b, i: (b, i, 0, 0)), full(f1s),
                  pl.BlockSpec((cs2, n1, LANES), lambda b, i: (i, 0, 0)),
                  pl.BlockSpec((cs2, n1, LANES), lambda b, i: (i, 0, 0))],
        out_specs=pl.BlockSpec((1, cs2, 2, n1, c), lambda b, i: (b, i, 0, 0, 0)),
        compiler_params=_cparams("arbitrary", "arbitrary"),
        name="dft_stage_a",
    )(vp, f1s, tc, ts)


def _spec_kernel(a_ref, mf_ref, inv_ref, o_ref):
    n2 = a_ref.shape[3]
    c = a_ref.shape[4]
    inv = inv_ref[...]
    for j in range(a_ref.shape[1]):
        yf = _dot(mf_ref[...], a_ref[0, j].reshape(2 * n2, c))
        yb = _dot(mf_ref[...], a_ref[1, j].reshape(2 * n2, c))
        o_ref[j, 0] = (yf[0:n2] + yb[0:n2]) * inv
        o_ref[j, 1] = (yf[n2:] - yb[n2:]) * inv


def _spec_call(at, mf, inv, cf):
    _, n1, _, n2, c = at.shape
    full = lambda a: pl.BlockSpec(a.shape, lambda i: (0,) * a.ndim)
    return pl.pallas_call(
        _spec_kernel,
        out_shape=jax.ShapeDtypeStruct((n1, 2, n2, c), F32),
        grid=(n1 // cf,),
        in_specs=[pl.BlockSpec((2, cf, 2, n2, c), lambda i: (0, i, 0, 0, 0)), full(mf), full(inv)],
        out_specs=pl.BlockSpec((cf, 2, n2, c), lambda i: (i, 0, 0, 0)),
        compiler_params=_cparams("arbitrary"),
        name="hyena_spectrum",
    )(at, mf, inv)


def _dft_b_kernel(a_ref, k_ref, mf_ref, mi_ref, tc_ref, ts_ref, o_ref):
    n2 = a_ref.shape[3]
    c = a_ref.shape[4]
    rep = c // LANES
    for j in range(a_ref.shape[1]):
        y = _dot(mf_ref[...], a_ref[0, j].reshape(2 * n2, c))
        yr, yi = y[0:n2], y[n2:]
        kr, ki = k_ref[j, 0], k_ref[j, 1]
        prod = jnp.concatenate([yr * kr - yi * ki, yr * ki + yi * kr], axis=0).astype(BF16)
        z = _dot(mi_ref[...], prod)
        zr, zi = z[0:n2], z[n2:]
        tc = jnp.concatenate([tc_ref[j]] * rep, axis=1)
        ts = jnp.concatenate([ts_ref[j]] * rep, axis=1)
        o_ref[0, j, 0] = (zr * tc - zi * ts).astype(BF16)
        o_ref[0, j, 1] = (zr * ts + zi * tc).astype(BF16)


def _dft_b_call(at, kspec, mf, mi, tc, ts, cf):
    nb, n1, _, n2, c = at.shape
    full = lambda a: pl.BlockSpec(a.shape, lambda b, i: (0,) * a.ndim)
    blk = pl.BlockSpec((1, cf, 2, n2, c), lambda b, i: (b, i, 0, 0, 0))
    tw = pl.BlockSpec((cf, n2, LANES), lambda b, i: (i, 0, 0))
    return pl.pallas_call(
        _dft_b_kernel,
        out_shape=jax.ShapeDtypeStruct((nb, n1, 2, n2, c), BF16),
        grid=(nb, n1 // cf),
        in_specs=[blk, pl.BlockSpec((cf, 2, n2, c), lambda b, i: (i, 0, 0, 0)), full(mf), full(mi), tw, tw],
        out_specs=blk,
        compiler_params=_cparams("arbitrary", "arbitrary"),
        name="dft_stage_b",
    )(at, kspec, mf, mi, tc, ts)


def _dft_c_kernel(z_ref, g_ref, vv_ref, x0_ref, bias_ref, o_ref):
    n1 = z_ref.shape[3]
    c = z_ref.shape[4]
    for j in range(z_ref.shape[1]):
        y = _dot(g_ref[...], z_ref[0, j].reshape(2 * n1, c))
        o_ref[0, j] = (y + vv_ref[0, j] * bias_ref[...]) * x0_ref[0, j]


def _dft_c_call(zt, g, vvp, x0p, bias, cs2):
    nb, n2, _, n1, c = zt.shape
    half = n1 // 2
    full = lambda a: pl.BlockSpec(a.shape, lambda b, i: (0,) * a.ndim)
    tok = pl.BlockSpec((1, cs2, half, c), lambda b, i: (b, i, 0, 0))
    return pl.pallas_call(
        _dft_c_kernel,
        out_shape=jax.ShapeDtypeStruct((nb, n2, half, c), F32),
        grid=(nb, n2 // cs2),
        in_specs=[pl.BlockSpec((1, cs2, 2, n1, c), lambda b, i: (b, i, 0, 0, 0)), full(g), tok, tok, full(bias)],
        out_specs=tok,
        compiler_params=_cparams("arbitrary", "arbitrary"),
        name="dft_stage_c",
    )(zt, g, vvp, x0p, bias)


def _swap_time(x, n2):
    b, l, c = x.shape
    return x.reshape(b, l // n2, n2, c).transpose(0, 2, 1, 3)


def _hyena_two_stage(vv, x0, hfb, inv, bias):
    b, l, c = vv.shape
    n2 = DFT_N2
    n = 2 * l
    n1 = n // n2
    half = n1 // 2
    cs2 = 8
    cf = 8
    c1, s1 = _cis_tables(n1, half, n1)
    f1s = jnp.concatenate([c1, -s1], axis=0).astype(BF16)
    c2, s2 = _cis_tables(n2, n2, n2)
    mf = jnp.concatenate([jnp.concatenate([c2, s2], 1), jnp.concatenate([-s2, c2], 1)], 0).astype(BF16)
    mi = jnp.concatenate([jnp.concatenate([c2, -s2], 1), jnp.concatenate([s2, c2], 1)], 0).astype(BF16)
    gc, gs = _cis_tables(half, n1, n1)
    g = (jnp.concatenate([gc, -gs], axis=1) * (1.0 / n)).astype(BF16)
    twa = [_lane_rep(t) for t in _cis_tables(n2, n1, n)]
    twb = [_lane_rep(t) for t in _cis_tables(n1, n2, n)]
    flip = lambda a: a.transpose(0, 3, 2, 1, 4)
    ka = _dft_a_call(_swap_time(hfb, n2), f1s, twa[0], twa[1], cs2)
    kspec = _spec_call(flip(ka), mf, inv, cf)
    vvp = _swap_time(vv, n2)
    a = _dft_a_call(vvp, f1s, twa[0], twa[1], cs2)
    z = _dft_b_call(flip(a), kspec, mf, mi, twb[0], twb[1], cf)
    yp = _dft_c_call(flip(z), g, vvp, _swap_time(x0, n2), bias, cs2)
    return yp.transpose(0, 2, 1, 3).reshape(b, l, c)


def _dense_conv_kernel(v_ref, x0_ref, h_ref, inv_ref, bias_ref, fd_ref, gi_ref, o_ref):
    n = fd_ref.shape[0] // 2
    fd = fd_ref[...]
    hf = _dot(fd, h_ref[0].astype(BF16))
    hb = _dot(fd, h_ref[1].astype(BF16))
    inv = inv_ref[...]
    kr = (hf[0:n] + hb[0:n]) * inv
    ki = (hf[n:] - hb[n:]) * inv
    v = v_ref[0]
    x = _dot(fd, v.astype(BF16))
    xr, xi = x[0:n], x[n:]
    prod = jnp.concatenate([xr * kr - xi * ki, xr * ki + xi * kr], axis=0).astype(BF16)
    y = _dot(gi_ref[...], prod)
    o_ref[0] = (y + v * bias_ref[...]) * x0_ref[0]


def _hyena_dense(vv, x0, hfb, inv, bias):
    b, l, c = vv.shape
    n = 2 * l
    cf, sf = _cis_tables(n, l, n)
    fd = jnp.concatenate([cf, -sf], axis=0).astype(BF16)
    ci, si = _cis_tables(l, n, n)
    gi = (jnp.concatenate([ci, -si], axis=1) * (1.0 / n)).astype(BF16)
    full = lambda a: pl.BlockSpec(a.shape, lambda bi: (0,) * a.ndim)
    tok = pl.BlockSpec((1, l, c), lambda bi: (bi, 0, 0))
    return pl.pallas_call(
        _dense_conv_kernel,
        out_shape=jax.ShapeDtypeStruct((b, l, c), F32),
        grid=(b,),
        in_specs=[tok, tok, full(hfb), full(inv), full(bias), full(fd), full(gi)],
        out_specs=tok,
        compiler_params=_cparams("arbitrary"),
        name="hyena_dense",
    )(vv, x0, hfb, inv, bias, fd, gi)


def _hyena(vv, x0, p):
    l = vv.shape[1]
    w1p = jnp.zeros((LANES, HY_FFN), F32).at[0:HY_EMB].set(p['hy_w1'])
    row = lambda a: a.reshape(1, -1)
    hfb, norm = _filter_call(w1p, row(p['hy_b1']), p['hy_w2'], row(p['hy_b2']), p['hy_w3'], row(p['hy_freq']),
                             row(p['hy_decay']), l, min(l, 512))
    inv = 1.0 / norm
    bias = row(p['hy_bias'])
    if l >= 8 * DFT_N2:
        return _hyena_two_stage(vv, x0, hfb, inv, bias)
    return _hyena_dense(vv, x0, hfb, inv, bias)


def _post_kernel(att_ref, lru_ref, hy_ref, x_ref, mod_ref, og_ref, wo_ref, fg_ref, rwh_ref, rwl_ref, rb_ref, cnt0_ref,
                 x1_ref, f_ref, sel_ref, gate_ref, rank_ref, cnt_ref, cnt_sc, *, ctx_row):
    row = pl.program_id(0) if ctx_row is None else ctx_row

    @pl.when((pl.program_id(0) == 0) & (pl.program_id(1) == 0))
    def _():
        cnt_sc[...] = cnt0_ref[...]

    d = x_ref.shape[2]
    tm = x_ref.shape[1]
    a_w, l_w = ATT_WIDTH, LRU_WIDTH
    o = (_dot(_rms(att_ref[0], og_ref[:, 0:a_w]).astype(BF16), wo_ref[0:a_w, :])
         + _dot(_rms(lru_ref[0], og_ref[:, a_w:a_w + l_w]).astype(BF16), wo_ref[a_w:a_w + l_w, :])
         + _dot(_rms(hy_ref[0], og_ref[:, a_w + l_w:]).astype(BF16), wo_ref[a_w + l_w:, :]))
    mod = lambda k: mod_ref[pl.ds(row, 1), k * d:(k + 1) * d]
    x1 = x_ref[0] + mod(2) * o
    x1_ref[0] = x1
    f = _rms(x1, fg_ref[...]) * (1.0 + mod(4)) + mod(3)
    f_ref[0] = f
    fh, fl = _split(f)
    logits = _dot(fh, rwh_ref[...]) + (_dot(fl, rwh_ref[...]) + _dot(fh, rwl_ref[...])) + rb_ref[...]
    lane = lax.broadcasted_iota(I32, (tm, LANES), 1)
    sel = jnp.zeros((tm, LANES), I32)
    vals = jnp.zeros((tm, LANES), F32)
    onehots = []
    for k in range(TOP_K):
        m = jnp.max(logits, axis=1, keepdims=True)
        idx = jnp.min(jnp.where(logits == m, lane, LANES), axis=1, keepdims=True)
        hit = lane == idx
        onehots.append(hit.astype(F32))
        sel = jnp.where(lane == k, idx, sel)
        vals = jnp.where(lane == k, m, vals)
        logits = jnp.where(hit, -jnp.inf, logits)
    e = jnp.where(lane < TOP_K, jnp.exp(vals - vals[:, 0:1]), 0.0)
    sel_ref[0] = sel
    gate_ref[0] = e / jnp.sum(e, axis=1, keepdims=True)
    picked = onehots[0] + onehots[1] + onehots[2] + onehots[3]
    earlier = lax.broadcasted_iota(I32, (tm, tm), 0) > lax.broadcasted_iota(I32, (tm, tm), 1)
    before = _dot(earlier.astype(BF16), picked.astype(BF16)) + cnt_sc[...]
    rank = jnp.zeros((tm, LANES), I32)
    for k in range(TOP_K):
        rk = jnp.sum(onehots[k] * before, axis=1, keepdims=True)
        rank = jnp.where(lane == k, rk.astype(I32), rank)
    rank_ref[0] = rank
    cnt_sc[...] += jnp.sum(picked, axis=0, keepdims=True)
    cnt_ref[...] = cnt_sc[...]


def _post_call(att, lru, hyo, x, mod_l, og, wo, fg, rwh, rwl, rb, cnt0, ctx_row, tm):
    b, l, d = x.shape
    full = lambda a: pl.BlockSpec(a.shape, lambda bi, i: (0,) * a.ndim)
    tok = lambda w: pl.BlockSpec((1, tm, w), lambda bi, i: (bi, i, 0))
    consts = [mod_l, og, wo, fg, rwh, rwl, rb, cnt0]
    return pl.pallas_call(
        functools.partial(_post_kernel, ctx_row=ctx_row),
        out_shape=(jax.ShapeDtypeStruct((b, l, d), F32), jax.ShapeDtypeStruct((b, l, d), F32),
                   jax.ShapeDtypeStruct((b, l, LANES), I32), jax.ShapeDtypeStruct((b, l, LANES), F32),
                   jax.ShapeDtypeStruct((b, l, LANES), I32), jax.ShapeDtypeStruct((1, LANES), F32)),
        grid=(b, l // tm),
        in_specs=[tok(ATT_WIDTH), tok(LRU_WIDTH), tok(HY_WIDTH), tok(d)] + [full(a) for a in consts],
        out_specs=(tok(d), tok(d), tok(LANES), tok(LANES), tok(LANES), pl.BlockSpec((1, LANES), lambda bi, i: (0, 0))),
        scratch_shapes=[pltpu.VMEM((1, LANES), F32)],
        compiler_params=_cparams("arbitrary", "arbitrary"),
        name="post_lat" if ctx_row is None else "post_ctx",
    )(att, lru, hyo, x, *consts)


def _route(sel, rank, counts, tm, n_tiles):
    t = sel.shape[0]
    e_ids = jnp.arange(N_EXPERTS, dtype=I32)
    tiles_per = (counts + tm - 1) // tm
    tile_end = jnp.cumsum(tiles_per)
    first_tile = tile_end - tiles_per
    grp_start = jnp.cumsum(counts) - counts
    pos = jnp.sum(jnp.where(sel[:, :, None] == e_ids, first_tile * tm, 0), axis=-1) + rank
    tile_ids = jnp.arange(n_tiles, dtype=I32)
    tile_expert = jnp.minimum(jnp.sum((tile_end[None, :] <= tile_ids[:, None]).astype(I32), axis=1), N_EXPERTS - 1)
    of_tile = lambda v: jnp.sum(jnp.where(tile_expert[:, None] == e_ids, v, 0), axis=1)
    tile_off = jnp.clip(of_tile(grp_start) + (tile_ids - of_tile(first_tile)) * tm, 0, TOP_K * t)
    n_used = tile_end[N_EXPERTS - 1:N_EXPERTS]
    tok = jnp.arange(TOP_K * t, dtype=I32) // TOP_K
    _, tok_sorted = lax.sort((sel.reshape(-1), tok), num_keys=1, is_stable=True)
    n_win = TOP_K * t // IDX_STEP + 2
    padded = jnp.zeros(((n_win + 1) * IDX_STEP,), I32).at[0:TOP_K * t].set(tok_sorted).reshape(n_win + 1, IDX_STEP)
    windows = jnp.concatenate([padded[:-1], padded[1:]], axis=1)
    return pos.astype(I32), tile_expert, tile_off.astype(I32), n_used.astype(I32), windows


def _moe_kernel(te_ref, toff_ref, nu_ref, win_hbm, f_hbm, perm_ref, w1_ref, b1g_ref, b1l_ref, w2_ref, b2_ref, o_ref,
                idx_sm, xbuf, w1g_sc, w1l_sc, w2_sc, sem_idx, sem_row):
    i = pl.program_id(0)
    tm = xbuf.shape[1]
    n_used = nu_ref[0]
    slot = i % 2

    def idx_copy(tile, s):
        win = lax.shift_right_logical(toff_ref[tile], IDX_STEP.bit_length() - 1)
        return pltpu.make_async_copy(win_hbm.at[win], idx_sm.at[s], sem_idx.at[s])

    def start_rows(tile, s):
        delta = toff_ref[tile] & (IDX_STEP - 1)
        for r in range(tm):
            pltpu.make_async_copy(f_hbm.at[pl.ds(idx_sm[s, delta + r], 1), :], xbuf.at[s, pl.ds(r, 1), :],
                                  sem_row.at[s]).start()

    @pl.when(i == 0)
    def _():
        idx_copy(0, 0).start()
        idx_copy(0, 0).wait()
        start_rows(0, 0)

        @pl.when(n_used > 1)
        def _():
            idx_copy(1, 1).start()

    for s in range(2):
        @pl.when((i + 1 < n_used) & (slot == 1 - s))
        def _(s=s):
            idx_copy(i + 1, s).wait()
            start_rows(i + 1, s)

            @pl.when(i + 2 < n_used)
            def _():
                idx_copy(i + 2, 1 - s).start()

    new_expert = (i == 0) | (te_ref[i] != te_ref[jnp.maximum(i - 1, 0)])

    @pl.when(new_expert & (i < n_used))
    def _():
        for jb in range(w1_ref.shape[2] // (2 * LANES)):
            blk = w1_ref[0, :, 2 * jb * LANES:2 * (jb + 1) * LANES].astype(BF16)
            de = _dot(blk, perm_ref[...])
            w1g_sc[:, jb * LANES:(jb + 1) * LANES] = de[:, 0:LANES].astype(BF16)
            w1l_sc[:, jb * LANES:(jb + 1) * LANES] = de[:, LANES:].astype(BF16)
        w2_sc[...] = w2_ref[0].astype(BF16)

    @pl.when(i < n_used)
    def _():
        pltpu.make_async_copy(f_hbm.at[pl.ds(0, tm), :], xbuf.at[slot], sem_row.at[slot]).wait()
        xb = xbuf[slot].astype(BF16)
        glu = jnp.minimum(_dot(xb, w1g_sc[...]) + b1g_ref[0], SWIGLU_LIMIT)
        lin = jnp.clip(_dot(xb, w1l_sc[...]) + b1l_ref[0], -SWIGLU_LIMIT, SWIGLU_LIMIT)
        act = glu * _sigmoid(SWIGLU_ALPHA * glu) * (lin + 1.0)
        o_ref[...] = _dot(act.astype(BF16), w2_sc[...]) + b2_ref[0]

    @pl.when(i >= n_used)
    def _():
        o_ref[...] = jnp.zeros(o_ref.shape, F32)


def _expert_weights(w1, b1, w2, b2):
    return w1, b1[:, None, 0::2], b1[:, None, 1::2], w2, b2[:, None, :]


def _moe_call(tile_expert, tile_off, n_used, windows, f, w1, b1g, b1l, w2, b2, tm):
    n_tiles = tile_expert.shape[0]
    d = f.shape[1]
    dff = w2.shape[1]
    k = jnp.arange(2 * LANES)
    perm = (jnp.where(k % 2 == 0, k // 2, LANES + k // 2)[:, None] == k[None, :]).astype(BF16)
    wspec = lambda shape: pl.BlockSpec((1,) + shape, lambda i, te, toff, nu: (te[i], 0, 0))
    grid_spec = pltpu.PrefetchScalarGridSpec(
        num_scalar_prefetch=3,
        grid=(n_tiles,),
        in_specs=[pl.BlockSpec(memory_space=pl.ANY), pl.BlockSpec(memory_space=pl.ANY),
                  pl.BlockSpec(perm.shape, lambda i, te, toff, nu: (0, 0)),
                  wspec((d, 2 * dff)), wspec((1, dff)), wspec((1, dff)), wspec((dff, d)), wspec((1, d))],
        out_specs=pl.BlockSpec((tm, d), lambda i, te, toff, nu: (i, 0)),
        scratch_shapes=[pltpu.SMEM((2, IDX_WINDOW), I32), pltpu.VMEM((2, tm, d), F32),
                        pltpu.VMEM((d, dff), BF16), pltpu.VMEM((d, dff), BF16), pltpu.VMEM((dff, d), BF16),
                        pltpu.SemaphoreType.DMA((2,)), pltpu.SemaphoreType.DMA((2,))],
    )
    return pl.pallas_call(
        _moe_kernel,
        out_shape=jax.ShapeDtypeStruct((n_tiles * tm, d), F32),
        grid_spec=grid_spec,
        compiler_params=_cparams("arbitrary"),
        name="moe_experts",
    )(tile_expert, tile_off, n_used, windows, f, perm, w1, b1g, b1l, w2, b2)


def _combine_kernel(pos_hbm, y_hbm, gate_ref, x1_ref, mod_ref, o_ref, idx_sm, ybuf, sem_idx, sem_row,
                    *, lat_tiles, tiles_per_batch, ctx_row):
    i = pl.program_id(0)
    n = pl.num_programs(0)
    tc = x1_ref.shape[0]
    d = x1_ref.shape[1]
    slot = i % 2

    def idx_copy(tile, s):
        return pltpu.make_async_copy(pos_hbm.at[tile], idx_sm.at[s], sem_idx.at[s])

    def start_rows(s):
        for r in range(TOP_K * tc):
            pltpu.make_async_copy(y_hbm.at[pl.ds(idx_sm[s, r], 1), :], ybuf.at[s, pl.ds(r, 1), :],
                                  sem_row.at[s]).start()

    @pl.when(i == 0)
    def _():
        idx_copy(0, 0).start()
        idx_copy(0, 0).wait()
        start_rows(0)

        @pl.when(n > 1)
        def _():
            idx_copy(1, 1).start()

    for s in range(2):
        @pl.when((i + 1 < n) & (slot == 1 - s))
        def _(s=s):
            idx_copy(i + 1, s).wait()
            start_rows(s)

            @pl.when(i + 2 < n)
            def _():
                idx_copy(i + 2, 1 - s).start()

    pltpu.make_async_copy(y_hbm.at[pl.ds(0, TOP_K * tc), :], ybuf.at[slot], sem_row.at[slot]).wait()
    row = jnp.where(i < lat_tiles, i // tiles_per_batch, ctx_row)
    gf = mod_ref[pl.ds(row, 1), 5 * d:6 * d]
    acc = None
    for j in range(TOP_K):
        term = gate_ref[:, j:j + 1] * ybuf[slot, j * tc:(j + 1) * tc, :]
        acc = term if acc is None else acc + term
    o_ref[...] = x1_ref[...] + gf * acc


def _combine_call(pos_tiles, y, gate, x1, mod_l, lat_tiles, tiles_per_batch, ctx_row, tc):
    t, d = x1.shape
    return pl.pallas_call(
        functools.partial(_combine_kernel, lat_tiles=lat_tiles, tiles_per_batch=tiles_per_batch, ctx_row=ctx_row),
        out_shape=jax.ShapeDtypeStruct((t, d), F32),
        grid=(t // tc,),
        in_specs=[pl.BlockSpec(memory_space=pl.ANY), pl.BlockSpec(memory_space=pl.ANY),
                  pl.BlockSpec((tc, LANES), lambda i: (i, 0)), pl.BlockSpec((tc, d), lambda i: (i, 0)),
                  pl.BlockSpec(mod_l.shape, lambda i: (0, 0))],
        out_specs=pl.BlockSpec((tc, d), lambda i: (i, 0)),
        scratch_shapes=[pltpu.SMEM((2, TOP_K * tc), I32), pltpu.VMEM((2, TOP_K * tc, d), F32),
                        pltpu.SemaphoreType.DMA((2,)), pltpu.SemaphoreType.DMA((2,))],
        compiler_params=_cparams("arbitrary"),
        name="moe_combine",
    )(pos_tiles, y, gate, x1, mod_l)


def _moe(f, sel, gate, rank, counts, x1, mod_l, ew, lat_tiles_rows, seq, ctx_row):
    t, d = f.shape
    tm = MOE_TILE
    tc = 128
    n_tiles = TOP_K * t // tm + N_EXPERTS
    pos, tile_expert, tile_off, n_used, windows = _route(sel[:, 0:TOP_K], rank[:, 0:TOP_K], counts, tm, n_tiles)
    y = _moe_call(tile_expert, tile_off, n_used, windows, f, *ew, tm)
    pos_tiles = pos.reshape(t // tc, tc, TOP_K).transpose(0, 2, 1).reshape(t // tc, TOP_K * tc)
    return _combine_call(pos_tiles, y, gate, x1, mod_l, lat_tiles_rows // tc, seq // tc, ctx_row, tc)


def _rope_tables(s):
    rows = (jnp.arange(s, dtype=I32) // GRID_W).astype(F32)
    cols = (jnp.arange(s, dtype=I32) % GRID_W).astype(F32)
    nfreq = HEAD_DIM // 4
    inv = ROPE_THETA ** (-jnp.arange(nfreq, dtype=F32) / nfreq)
    ang = jnp.concatenate([rows[:, None] * inv, cols[:, None] * inv], axis=-1)
    cos, sin = jnp.cos(ang), jnp.sin(ang)
    reps = LANES // HEAD_DIM
    return jnp.tile(jnp.concatenate([cos, cos], -1), (1, reps)), jnp.tile(jnp.concatenate([-sin, sin], -1), (1, reps))


def _pick_tk(lk):
    for k in (768, 512, 256, 128):
        if lk % k == 0:
            return k
    return lk


def kernel(x, c, ctx, c_ctx, w_mod, b_mod, norm_mix_g, norm_ffn_g, w_in, q_norm_g, k_norm_g, lru_conv_w, lru_conv_b, lru_gate_w, lru_gate_b, lru_lambda, hy_conv_w, hy_conv_b, hy_w1, hy_b1, hy_w2, hy_b2, hy_w3, hy_freq, hy_decay, hy_bias, out_norm_g, w_out, router_w, router_b, moe_w1, moe_b1, moe_w2, moe_b2):
    b, s, d = x.shape
    lc = ctx.shape[1]
    depth = w_mod.shape[0]
    ctx_row = b
    assert b < SUBLANES and d == D_MODEL
    c8 = jnp.concatenate([c, c_ctx[None, :], jnp.zeros((SUBLANES - b - 1, d), F32)], axis=0)
    mods = _mod_call(c8, w_mod, b_mod)
    rope_tabs = _rope_tables(s)
    lane = jnp.arange(LANES)
    bd = ((lane[:, None] // HEAD_DIM) == (lane[None, :] // HEAD_DIM)).astype(F32) * (1.0 / HEAD_DIM)
    bd = bd.astype(BF16)
    row = lambda a: a.reshape(1, -1)
    tile_heads = lambda g: jnp.tile(g.reshape(1, HEAD_DIM), (1, LANES // HEAD_DIM))
    tm = min(512, s)
    tmc = min(256, lc)
    t_lat = b * s

    for l in range(depth):
        last = l == depth - 1
        mod_l = mods[l]
        w_in_l = w_in[l].astype(BF16)
        qg, kg = tile_heads(q_norm_g[l]), tile_heads(k_norm_g[l])
        p = {'hy_w1': hy_w1[l], 'hy_b1': hy_b1[l], 'hy_w2': hy_w2[l], 'hy_b2': hy_b2[l], 'hy_w3': hy_w3[l],
             'hy_freq': hy_freq[l], 'hy_decay': hy_decay[l], 'hy_bias': hy_bias[l]}
        conv = (lru_conv_w[l], row(lru_conv_b[l]), hy_conv_w[l], row(hy_conv_b[l]))

        q, k, v, rest = _in_proj_call(x, mod_l, row(norm_mix_g[l]), w_in_l, qg, kg, bd, rope_tabs, None, tm)
        qc, kc, vc, rest_c = _in_proj_call(ctx, mod_l, row(norm_mix_g[l]), w_in_l, qg, kg, bd, None, ctx_row, tmc)
        xl, vv, x0 = _prep_call(rest, *conv, tm)
        xl_c, vv_c, x0_c = _prep_call(rest_c, *conv, tmc)

        kd = _dup_heads(jnp.concatenate([kc, k], axis=1))
        vd = _dup_heads(jnp.concatenate([vc, v], axis=1))
        att = _attn_call(q, kd, vd, min(256, s), _pick_tk(lc + s))

        zeros_h = jnp.zeros((b, 1, LRU_WIDTH), F32)
        gws = [_lru_gate_weights(lru_gate_w[l, dr], lru_gate_b[l, dr]) for dr in range(2)]
        lams = [row(lru_lambda[l, dr]) for dr in range(2)]
        hc_f = _lru_call(xl_c, *gws[0], lams[0], zeros_h, False, tmc)
        hc_b, lru_c = _lru_call(xl_c, *gws[1], lams[1], zeros_h, True, tmc, hf=hc_f, rest=rest_c)
        h_f = _lru_call(xl, *gws[0], lams[0], hc_f[:, lc - 1:lc], False, tm)
        _, lru = _lru_call(xl, *gws[1], lams[1], hc_b[:, 0:1], True, tm, hf=h_f, rest=rest)

        hyo = _hyena(vv, x0, p)

        rw = jnp.zeros((d, LANES), F32).at[:, 0:N_EXPERTS].set(router_w[l])
        rwh, rwl = _split(rw)
        rb = jnp.full((1, LANES), -1e30, F32).at[0, 0:N_EXPERTS].set(router_b[l])
        post_consts = (row(out_norm_g[l]), w_out[l].astype(BF16), row(norm_ffn_g[l]), rwh, rwl, rb)
        no_picks = jnp.zeros((1, LANES), F32)
        x1, f, sel, gate, rank, cnt = _post_call(att, lru, hyo, x, mod_l, *post_consts, no_picks, None, tm)
        ew = _expert_weights(moe_w1[l], moe_b1[l], moe_w2[l], moe_b2[l])
        flat = lambda a: a.reshape(-1, a.shape[-1])
        counts = lambda cn: cn[0, 0:N_EXPERTS].astype(I32)
        if last:
            x = _moe(flat(f), flat(sel), flat(gate), flat(rank), counts(cnt), flat(x1), mod_l, ew, t_lat, s,
                     ctx_row).reshape(b, s, d)
        else:
            att_c = _attn_call(qc, _dup_heads(kc), _dup_heads(vc), tmc, _pick_tk(lc))
            hyo_c = _hyena(vv_c, x0_c, p)
            x1c, fc, selc, gatec, rankc, cnt = _post_call(att_c, lru_c, hyo_c, ctx, mod_l, *post_consts, cnt, ctx_row,
                                                         tmc)
            cat = lambda a, bb: jnp.concatenate([flat(a), flat(bb)], axis=0)
            out = _moe(cat(f, fc), cat(sel, selc), cat(gate, gatec), cat(rank, rankc), counts(cnt), cat(x1, x1c),
                       mod_l, ew, t_lat, s, ctx_row)
            x = out[0:t_lat].reshape(b, s, d)
            ctx = out[t_lat:].reshape(b, lc, d)
    return x
```

```python
import functools
import math

import jax
import jax.numpy as jnp
from jax import lax
from jax.experimental import pallas as pl
from jax.experimental.pallas import tpu as pltpu

F32 = jnp.float32
BF16 = jnp.bfloat16
I32 = jnp.int32

D_MODEL = 1024
HEAD_DIM = 64
ATT_WIDTH = 512
KV_WIDTH = 128
LRU_WIDTH = 256
HY_WIDTH = 256
REST_WIDTH = 2 * LRU_WIDTH + 3 * HY_WIDTH
IN_WIDTH = ATT_WIDTH + 2 * KV_WIDTH + REST_WIDTH
GRID_W = 64
ROPE_THETA = 10000.0
LRU_C = 8.0
HY_EMB = 33
HY_BANDS = 16
HY_FFN = 64
N_EXPERTS = 32
TOP_K = 4
D_FF = 1024
SWIGLU_ALPHA = 1.702
SWIGLU_LIMIT = 7.0
EPS = 1e-6
LANES = 128
SUBLANES = 8
DFT_N2 = 128
VMEM_LIMIT = 56 * 1024 * 1024
MOE_TILE = 256
IDX_STEP = 1024
IDX_WINDOW = 2 * IDX_STEP
Q_SCALE = HEAD_DIM ** -0.5 * math.log2(math.e)


def _cparams(*sem):
    return pltpu.CompilerParams(dimension_semantics=sem, vmem_limit_bytes=VMEM_LIMIT)


def _dot(a, b):
    return jnp.dot(a, b, preferred_element_type=F32)


def _split(x):
    hi = x.astype(BF16)
    return hi, (x - hi.astype(F32)).astype(BF16)


def _dot3(a, b):
    ah, al = _split(a)
    bh, bl = _split(b)
    return _dot(ah, bh) + (_dot(al, bh) + _dot(ah, bl))


def _sigmoid(x):
    return 1.0 / (1.0 + jnp.exp(-x))


def _rms(x, g):
    return x * lax.rsqrt(jnp.mean(x * x, axis=-1, keepdims=True) + EPS) * g


def _mod_kernel(c_ref, w_ref, b_ref, o_ref):
    c = c_ref[...]
    o_ref[0] = _dot3(c * _sigmoid(c), w_ref[0]) + b_ref[0]


def _mod_call(c8, w_mod, b_mod):
    depth, d, n6 = w_mod.shape
    tn = n6 // 4
    return pl.pallas_call(
        _mod_kernel,
        out_shape=jax.ShapeDtypeStruct((depth, SUBLANES, n6), F32),
        grid=(depth, n6 // tn),
        in_specs=[pl.BlockSpec((SUBLANES, d), lambda l, j: (0, 0)),
                  pl.BlockSpec((1, d, tn), lambda l, j: (l, 0, j)),
                  pl.BlockSpec((1, 1, tn), lambda l, j: (l, 0, j))],
        out_specs=pl.BlockSpec((1, SUBLANES, tn), lambda l, j: (l, 0, j)),
        compiler_params=_cparams("arbitrary", "arbitrary"),
        name="mod",
    )(c8, w_mod, b_mod.reshape(depth, 1, n6))


def _head_norm(x, g, bd):
    hi, lo = _split(x * x)
    ms = _dot(hi, bd) + _dot(lo, bd)
    return x * lax.rsqrt(ms + EPS) * g


def _rope(y, cos, sin, first_half):
    swapped = jnp.where(first_half, pltpu.roll(y, LANES - HEAD_DIM // 2, 1), pltpu.roll(y, HEAD_DIM // 2, 1))
    return y * cos + swapped * sin


def _in_proj_kernel(*refs, rope, ctx_row):
    if rope:
        x_ref, mod_ref, g_ref, w_ref, qg_ref, kg_ref, bd_ref, cos_ref, sin_ref, q_ref, k_ref, v_ref, r_ref = refs
    else:
        x_ref, mod_ref, g_ref, w_ref, qg_ref, kg_ref, bd_ref, q_ref, k_ref, v_ref, r_ref = refs
    row = pl.program_id(0) if ctx_row is None else ctx_row
    d = x_ref.shape[2]
    tm = x_ref.shape[1]
    shift = mod_ref[pl.ds(row, 1), 0:d]
    scale = mod_ref[pl.ds(row, 1), d:2 * d]
    h = _rms(x_ref[0], g_ref[...]) * (1.0 + scale) + shift
    u = _dot(h.astype(BF16), w_ref[...])
    bd = bd_ref[...]
    if rope:
        cos = cos_ref[...]
        sin = sin_ref[...]
        lane = lax.broadcasted_iota(I32, (tm, LANES), 1)
        first_half = (lane & (HEAD_DIM // 2)) == 0
    for p in range(ATT_WIDTH // LANES):
        y = _head_norm(u[:, p * LANES:(p + 1) * LANES], qg_ref[...], bd)
        if rope:
            y = _rope(y, cos, sin, first_half)
        q_ref[0, :, p * LANES:(p + 1) * LANES] = (y * Q_SCALE).astype(BF16)
    y = _head_norm(u[:, ATT_WIDTH:ATT_WIDTH + KV_WIDTH], kg_ref[...], bd)
    if rope:
        y = _rope(y, cos, sin, first_half)
    k_ref[0] = y.astype(BF16)
    v_ref[0] = u[:, ATT_WIDTH + KV_WIDTH:ATT_WIDTH + 2 * KV_WIDTH].astype(BF16)
    r_ref[0] = u[:, ATT_WIDTH + 2 * KV_WIDTH:]


def _in_proj_call(x, mod_l, g, w_in, qg, kg, bd, rope_tabs, ctx_row, tm):
    b, l, d = x.shape
    rope = rope_tabs is not None
    full = lambda shape: pl.BlockSpec(shape, lambda bi, i: (0,) * len(shape))
    in_specs = [pl.BlockSpec((1, tm, d), lambda bi, i: (bi, i, 0)),
                full(mod_l.shape), full((1, d)), full(w_in.shape), full((1, LANES)), full((1, LANES)),
                full((LANES, LANES))]
    args = [x, mod_l, g, w_in, qg, kg, bd]
    if rope:
        in_specs += [pl.BlockSpec((tm, LANES), lambda bi, i: (i, 0))] * 2
        args += list(rope_tabs)
    tok = lambda w: pl.BlockSpec((1, tm, w), lambda bi, i: (bi, i, 0))
    return pl.pallas_call(
        functools.partial(_in_proj_kernel, rope=rope, ctx_row=ctx_row),
        out_shape=(jax.ShapeDtypeStruct((b, l, ATT_WIDTH), BF16), jax.ShapeDtypeStruct((b, l, KV_WIDTH), BF16),
                   jax.ShapeDtypeStruct((b, l, KV_WIDTH), BF16), jax.ShapeDtypeStruct((b, l, REST_WIDTH), F32)),
        grid=(b, l // tm),
        in_specs=in_specs,
        out_specs=(tok(ATT_WIDTH), tok(KV_WIDTH), tok(KV_WIDTH), tok(REST_WIDTH)),
        compiler_params=_cparams("arbitrary", "arbitrary"),
        name="in_proj_lat" if rope else "in_proj_ctx",
    )(*args)


def _conv_taps(win, w, left, tm):
    acc = None
    for j in range(w.shape[0]):
        off = SUBLANES + j - left
        term = win[off:off + tm] * w[j:j + 1]
        acc = term if acc is None else acc + term
    return acc


def _prep_kernel(cur_ref, prev_ref, next_ref, lw_ref, lb_ref, hw_ref, hb_ref, xl_ref, vv_ref, x0_ref):
    i = pl.program_id(1)
    tm = cur_ref.shape[1]
    has_prev = (i > 0).astype(F32)
    has_next = (i < pl.num_programs(1) - 1).astype(F32)

    def window(lo, hi):
        return jnp.concatenate([prev_ref[0, :, lo:hi] * has_prev, cur_ref[0, :, lo:hi],
                                next_ref[0, :, lo:hi] * has_next], axis=0)

    xl_ref[0] = _conv_taps(window(0, LRU_WIDTH), lw_ref[...], 2, tm) + lb_ref[...]
    uc = _conv_taps(window(2 * LRU_WIDTH, REST_WIDTH), hw_ref[...], 1, tm) + hb_ref[...]
    x0_ref[0] = uc[:, 0:HY_WIDTH]
    vv_ref[0] = uc[:, 2 * HY_WIDTH:] * uc[:, HY_WIDTH:2 * HY_WIDTH]


def _prep_call(rest, lw, lb, hw, hb, tm):
    b, l, w = rest.shape
    nb = tm // SUBLANES
    last = l // SUBLANES - 1
    full = lambda a: pl.BlockSpec(a.shape, lambda bi, i: (0,) * a.ndim)
    tok = pl.BlockSpec((1, tm, LRU_WIDTH), lambda bi, i: (bi, i, 0))
    out = jax.ShapeDtypeStruct((b, l, LRU_WIDTH), F32)
    return pl.pallas_call(
        _prep_kernel,
        out_shape=(out, out, out),
        grid=(b, l // tm),
        in_specs=[pl.BlockSpec((1, tm, w), lambda bi, i: (bi, i, 0)),
                  pl.BlockSpec((1, SUBLANES, w), lambda bi, i: (bi, jnp.maximum(i * nb - 1, 0), 0)),
                  pl.BlockSpec((1, SUBLANES, w), lambda bi, i: (bi, jnp.minimum((i + 1) * nb, last), 0)),
                  full(lw), full(lb), full(hw), full(hb)],
        out_specs=(tok, tok, tok),
        compiler_params=_cparams("arbitrary", "arbitrary"),
        name="conv_prep",
    )(rest, rest, rest, lw, lb, hw, hb)


def _attn_kernel(q_ref, k_ref, v_ref, o_ref, qst_sc, m_sc, acc_sc, pa_sc, pb_sc, aa_sc, ab_sc, *, tk):
    tq = q_ref.shape[1]
    lk = k_ref.shape[2]
    nl = tk // LANES
    low = lax.broadcasted_iota(I32, (tq, LANES), 1) < HEAD_DIM
    for g in range(2):
        for n, p in enumerate((2 * g, 2 * g + 1)):
            qs = q_ref[0, :, p * LANES:(p + 1) * LANES]
            zero = jnp.zeros_like(qs)
            qst_sc[g, 2 * n * tq:(2 * n + 1) * tq, :] = jnp.where(low, qs, zero)
            qst_sc[g, (2 * n + 1) * tq:(2 * n + 2) * tq, :] = jnp.where(low, zero, qs)
    m_sc[...] = jnp.full(m_sc.shape, -jnp.inf, F32)
    acc_sc[...] = jnp.zeros(acc_sc.shape, F32)

    def scores(c, p_sc, a_sc):
        start = pl.multiple_of(c * tk, tk)
        for g in range(2):
            kc = k_ref[0, g, pl.ds(start, tk), :]
            s = lax.dot_general(qst_sc[g], kc, (((1,), (1,)), ((), ())), preferred_element_type=F32)
            cols = [s[:, n * LANES:(n + 1) * LANES] for n in range(nl)]
            mx = cols[0]
            for col in cols[1:]:
                mx = jnp.maximum(mx, col)
            m_old = m_sc[g]
            m_new = jnp.maximum(m_old, jnp.max(mx, axis=1, keepdims=True))
            a_sc[g] = jnp.exp2(m_old - m_new)
            p_sc[g] = jnp.concatenate([jnp.exp2(col - m_new).astype(BF16) for col in cols], axis=1)
            m_sc[g] = m_new

    def values(c, p_sc, a_sc):
        start = pl.multiple_of(c * tk, tk)
        for g in range(2):
            acc_sc[g] = a_sc[g] * acc_sc[g] + _dot(p_sc[g], v_ref[0, g, pl.ds(start, tk), :])

    nk = lk // tk
    scores(0, pa_sc, aa_sc)

    def body(i, carry):
        scores(2 * i + 1, pb_sc, ab_sc)
        values(2 * i, pa_sc, aa_sc)
        scores(2 * i + 2, pa_sc, aa_sc)
        values(2 * i + 1, pb_sc, ab_sc)
        return carry

    lax.fori_loop(0, (nk - 1) // 2, body, 0)
    if nk % 2:
        values(nk - 1, pa_sc, aa_sc)
    else:
        scores(nk - 1, pb_sc, ab_sc)
        values(nk - 2, pa_sc, aa_sc)
        values(nk - 1, pb_sc, ab_sc)
    for g in range(2):
        acc = acc_sc[g]
        o = acc / acc[:, HEAD_DIM:HEAD_DIM + 1]
        for n, p in enumerate((2 * g, 2 * g + 1)):
            o_ref[0, :, p * LANES:(p + 1) * LANES] = jnp.where(
                low, o[2 * n * tq:(2 * n + 1) * tq], pltpu.roll(o[(2 * n + 1) * tq:(2 * n + 2) * tq], HEAD_DIM, 1))


def _attn_call(q, kd, vd, tq, tk):
    b, l, _ = q.shape
    lk = kd.shape[2]
    kv = pl.BlockSpec((1, 2, lk, LANES), lambda bi, i: (bi, 0, 0, 0))
    return pl.pallas_call(
        functools.partial(_attn_kernel, tk=tk),
        out_shape=jax.ShapeDtypeStruct((b, l, ATT_WIDTH), F32),
        grid=(b, l // tq),
        in_specs=[pl.BlockSpec((1, tq, ATT_WIDTH), lambda bi, i: (bi, i, 0)), kv, kv],
        out_specs=pl.BlockSpec((1, tq, ATT_WIDTH), lambda bi, i: (bi, i, 0)),
        scratch_shapes=[pltpu.VMEM((2, 4 * tq, LANES), BF16), pltpu.VMEM((2, 4 * tq, LANES), F32),
                        pltpu.VMEM((2, 4 * tq, LANES), F32),
                        pltpu.VMEM((2, 4 * tq, tk), BF16), pltpu.VMEM((2, 4 * tq, tk), BF16),
                        pltpu.VMEM((2, 4 * tq, LANES), F32), pltpu.VMEM((2, 4 * tq, LANES), F32)],
        compiler_params=_cparams("arbitrary", "arbitrary"),
        name="attn",
    )(q, kd, vd)


def _dup_heads(x):
    halves = [jnp.concatenate([x[..., g * HEAD_DIM:(g + 1) * HEAD_DIM]] * 2, axis=-1) for g in range(2)]
    return jnp.stack(halves, axis=1)


def _val_heads(x):
    pad = jnp.zeros(x.shape[:-1] + (LANES - HEAD_DIM,), x.dtype).at[..., 0].set(1)
    return jnp.stack([jnp.concatenate([x[..., g * HEAD_DIM:(g + 1) * HEAD_DIM], pad], axis=-1) for g in range(2)], 1)


def _gelu_tanh(x):
    return 0.5 * x * (1.0 + jnp.tanh(math.sqrt(2.0 / math.pi) * (x + 0.044715 * (x * x * x))))


def _lru_kernel(*refs, reverse):
    if reverse:
        xl_ref, w_ref, b_ref, lam_ref, h0_ref, hf_ref, lg_ref, h_ref, o_ref, a_sc, b_sc, c_sc = refs
    else:
        xl_ref, w_ref, b_ref, lam_ref, h0_ref, h_ref, a_sc, b_sc, c_sc = refs
    tc = xl_ref.shape[1]
    nt = tc // SUBLANES

    @pl.when(pl.program_id(1) == 0)
    def _():
        c_sc[...] = jnp.broadcast_to(h0_ref[0], c_sc.shape)

    xc = xl_ref[0]
    gates = _dot(xc.astype(BF16), w_ref[...]) + b_ref[...]
    r = _sigmoid(gates[:, 0:LRU_WIDTH])
    gi = _sigmoid(gates[:, LRU_WIDTH:])
    nlam = -lam_ref[...]
    softplus = jnp.maximum(nlam, 0.0) + jnp.log1p(jnp.exp(-jnp.abs(nlam)))
    log_a = -LRU_C * r * softplus
    a = jnp.exp(log_a)
    a_sc[...] = a
    b_sc[...] = jnp.sqrt(1.0 - a * a) * (gi * xc)
    row = lax.broadcasted_iota(I32, (SUBLANES, LRU_WIDTH), 0)

    def tile(k, carry):
        kk = nt - 1 - k if reverse else k
        start = pl.multiple_of(kk * SUBLANES, SUBLANES)
        a = a_sc[pl.ds(start, SUBLANES), :]
        bv = b_sc[pl.ds(start, SUBLANES), :]
        for dist in (1, 2, 4):
            shift = SUBLANES - dist if reverse else dist
            keep = (row < SUBLANES - dist) if reverse else (row >= dist)
            a_n = pltpu.roll(a, shift, 0)
            b_n = pltpu.roll(bv, shift, 0)
            bv = jnp.where(keep, a * b_n + bv, bv)
            a = jnp.where(keep, a * a_n, a)
        h = a * c_sc[...] + bv
        h_ref[0, pl.ds(start, SUBLANES), :] = h
        edge = h[0:1] if reverse else h[SUBLANES - 1:SUBLANES]
        c_sc[...] = jnp.broadcast_to(edge, c_sc.shape)
        return carry

    lax.fori_loop(0, nt, tile, 0)
    if reverse:
        o_ref[0] = (hf_ref[0] + h_ref[0]) * _gelu_tanh(lg_ref[0])


def _lru_call(xl, w, bias, lam, h0, reverse, tc, hf=None, rest=None):
    b, l, c = xl.shape
    nc = l // tc
    chunk = (lambda bi, i: (bi, nc - 1 - i, 0)) if reverse else (lambda bi, i: (bi, i, 0))
    full = lambda a: pl.BlockSpec(a.shape, lambda bi, i: (0,) * a.ndim)
    tok = pl.BlockSpec((1, tc, c), chunk)
    in_specs = [tok, full(w), full(bias), full(lam), pl.BlockSpec((1, 1, c), lambda bi, i: (bi, 0, 0))]
    args = [xl, w, bias, lam, h0]
    out = jax.ShapeDtypeStruct((b, l, c), F32)
    if reverse:
        lg_chunk = (lambda bi, i: (bi, nc - 1 - i, 1))
        in_specs += [tok, pl.BlockSpec((1, tc, c), lg_chunk)]
        args += [hf, rest]
        out_shape, out_specs = (out, out), (tok, tok)
    else:
        out_shape, out_specs = out, tok
    return pl.pallas_call(
        functools.partial(_lru_kernel, reverse=reverse),
        out_shape=out_shape,
        grid=(b, nc),
        in_specs=in_specs,
        out_specs=out_specs,
        scratch_shapes=[pltpu.VMEM((tc, c), F32), pltpu.VMEM((tc, c), F32), pltpu.VMEM((SUBLANES, c), F32)],
        compiler_params=_cparams("arbitrary", "arbitrary"),
        name="lru_bwd" if reverse else "lru_fwd",
    )(*args)


def _lru_gate_weights(gw, gb):
    nb, bw = gw.shape[1], gw.shape[2]
    cols = []
    for j in range(2):
        m = jnp.zeros((nb * bw, nb * bw), F32)
        for n in range(nb):
            m = m.at[n * bw:(n + 1) * bw, n * bw:(n + 1) * bw].set(gw[j, n])
        cols.append(m)
    return jnp.concatenate(cols, axis=1).astype(BF16), gb.reshape(1, -1)


def _filter_kernel(w1_ref, b1_ref, w2_ref, b2_ref, w3_ref, fr_ref, dec_ref, h_ref, n_ref, *, seq):
    i = pl.program_id(0)
    tl = h_ref.shape[1]
    c = h_ref.shape[2]
    t = (i * tl + lax.broadcasted_iota(I32, (tl, LANES), 0)).astype(F32)
    lane = lax.broadcasted_iota(I32, (tl, LANES), 1)
    t01 = t * (1.0 / (seq - 1))
    kidx = jnp.where(lane <= HY_BANDS, lane - 1, lane - 1 - HY_BANDS).astype(F32)
    band = 1e-4 + kidx * ((HY_BANDS - 1 - 1e-4) / (HY_BANDS - 1))
    ang = (2.0 * math.pi / seq) * t * band
    z = jnp.where(lane == 0, t01,
                  jnp.where(lane <= HY_BANDS, jnp.cos(ang),
                            jnp.where(lane <= 2 * HY_BANDS, -jnp.sin(ang), 0.0)))
    fr = fr_ref[...]
    h = jnp.sin(fr * (_dot3(z, w1_ref[...]) + b1_ref[...]))
    h = jnp.sin(fr * (_dot3(h, w2_ref[...]) + b2_ref[...]))
    h = _dot3(h, w3_ref[...]) * jnp.exp(-t01[:, 0:1] * jnp.abs(dec_ref[...]))
    hf = h[:, 0:c]
    hb = jnp.where(t[:, 0:1] == 0.0, 0.0, h[:, c:])
    h_ref[0] = hf
    h_ref[1] = hb

    @pl.when(i == 0)
    def _():
        n_ref[...] = jnp.zeros(n_ref.shape, F32)

    n_ref[...] += jnp.sum(jnp.abs(hf) + jnp.abs(hb), axis=0, keepdims=True)


def _filter_call(w1p, b1, w2, b2, w3, fr, dec, seq, tl):
    c = w3.shape[1] // 2
    full = lambda a: pl.BlockSpec(a.shape, lambda i: (0,) * a.ndim)
    args = [w1p, b1, w2, b2, w3, fr, dec]
    return pl.pallas_call(
        functools.partial(_filter_kernel, seq=seq),
        out_shape=(jax.ShapeDtypeStruct((2, seq, c), F32), jax.ShapeDtypeStruct((1, c), F32)),
        grid=(seq // tl,),
        in_specs=[full(a) for a in args],
        out_specs=(pl.BlockSpec((2, tl, c), lambda i: (0, i, 0)), pl.BlockSpec((1, c), lambda i: (0, 0))),
        compiler_params=_cparams("arbitrary"),
        name="hyena_filter",
    )(*args)


def _cis_tables(na, nb, n):
    prod = (jnp.arange(na, dtype=I32)[:, None] * jnp.arange(nb, dtype=I32)[None, :]) % n
    ang = prod.astype(F32) * (2.0 * math.pi / n)
    return jnp.cos(ang), jnp.sin(ang)


def _lane_rep(x):
    return jnp.broadcast_to(x[:, :, None], x.shape + (LANES,))


def _dft_a_kernel(v_ref, f_ref, tc_ref, ts_ref, o_ref):
    n1 = tc_ref.shape[1]
    c = v_ref.shape[3]
    rep = c // LANES
    for j in range(v_ref.shape[1]):
        p_ = _dot(f_ref[...], v_ref[0, j].astype(BF16))
        pr, pi = p_[0:n1], p_[n1:]
        tc = jnp.concatenate([tc_ref[j]] * rep, axis=1)
        ts = jnp.concatenate([ts_ref[j]] * rep, axis=1)
        o_ref[0, j, 0] = (pr * tc + pi * ts).astype(BF16)
        o_ref[0, j, 1] = (pi * tc - pr * ts).astype(BF16)


def _dft_a_call(vp, f1s, tc, ts, cs2):
    nb, n2, half, c = vp.shape
    n1 = 2 * half
    full = lambda a: pl.BlockSpec(a.shape, lambda b, i: (0,) * a.ndim)
    return pl.pallas_call(
        _dft_a_kernel,
        out_shape=jax.ShapeDtypeStruct((nb, n2, 2, n1, c), BF16),
        grid=(nb, n2 // cs2),
        in_specs=[pl.BlockSpec((1, cs2, half, c), lambda b, i: (b, i, 0, 0)), full(f1s),
                  pl.BlockSpec((cs2, n1, LANES), lambda b, i: (i, 0, 0)),
                  pl.BlockSpec((cs2, n1, LANES), lambda b, i: (i, 0, 0))],
        out_specs=pl.BlockSpec((1, cs2, 2, n1, c), lambda b, i: (b, i, 0, 0, 0)),
        compiler_params=_cparams("arbitrary", "arbitrary"),
        name="dft_stage_a",
    )(vp, f1s, tc, ts)


def _spec_kernel(a_ref, mf_ref, inv_ref, o_ref):
    n2 = a_ref.shape[3]
    c = a_ref.shape[4]
    inv = inv_ref[...]
    for j in range(a_ref.shape[1]):
        yf = _dot(mf_ref[...], a_ref[0, j].reshape(2 * n2, c))
        yb = _dot(mf_ref[...], a_ref[1, j].reshape(2 * n2, c))
        o_ref[j, 0] = (yf[0:n2] + yb[0:n2]) * inv
        o_ref[j, 1] = (yf[n2:] - yb[n2:]) * inv


def _spec_call(at, mf, inv, cf):
    _, n1, _, n2, c = at.shape
    full = lambda a: pl.BlockSpec(a.shape, lambda i: (0,) * a.ndim)
    return pl.pallas_call(
        _spec_kernel,
        out_shape=jax.ShapeDtypeStruct((n1, 2, n2, c), F32),
        grid=(n1 // cf,),
        in_specs=[pl.BlockSpec((2, cf, 2, n2, c), lambda i: (0, i, 0, 0, 0)), full(mf), full(inv)],
        out_specs=pl.BlockSpec((cf, 2, n2, c), lambda i: (i, 0, 0, 0)),
        compiler_params=_cparams("arbitrary"),
        name="hyena_spectrum",
    )(at, mf, inv)


def _dft_b_kernel(a_ref, k_ref, mf_ref, mi_ref, tc_ref, ts_ref, o_ref):
    n2 = a_ref.shape[3]
    c = a_ref.shape[4]
    rep = c // LANES
    for j in range(a_ref.shape[1]):
        y = _dot(mf_ref[...], a_ref[0, j].reshape(2 * n2, c))
        yr, yi = y[0:n2], y[n2:]
        kr, ki = k_ref[j, 0], k_ref[j, 1]
        prod = jnp.concatenate([yr * kr - yi * ki, yr * ki + yi * kr], axis=0).astype(BF16)
        z = _dot(mi_ref[...], prod)
        zr, zi = z[0:n2], z[n2:]
        tc = jnp.concatenate([tc_ref[j]] * rep, axis=1)
        ts = jnp.concatenate([ts_ref[j]] * rep, axis=1)
        o_ref[0, j, 0] = (zr * tc - zi * ts).astype(BF16)
        o_ref[0, j, 1] = (zr * ts + zi * tc).astype(BF16)


def _dft_b_call(at, kspec, mf, mi, tc, ts, cf):
    nb, n1, _, n2, c = at.shape
    full = lambda a: pl.BlockSpec(a.shape, lambda b, i: (0,) * a.ndim)
    blk = pl.BlockSpec((1, cf, 2, n2, c), lambda b, i: (b, i, 0, 0, 0))
    tw = pl.BlockSpec((cf, n2, LANES), lambda b, i: (i, 0, 0))
    return pl.pallas_call(
        _dft_b_kernel,
        out_shape=jax.ShapeDtypeStruct((nb, n1, 2, n2, c), BF16),
        grid=(nb, n1 // cf),
        in_specs=[blk, pl.BlockSpec((cf, 2, n2, c), lambda b, i: (i, 0, 0, 0)), full(mf), full(mi), tw, tw],
        out_specs=blk,
        compiler_params=_cparams("arbitrary", "arbitrary"),
        name="dft_stage_b",
    )(at, kspec, mf, mi, tc, ts)


def _dft_c_kernel(z_ref, g_ref, vv_ref, x0_ref, bias_ref, o_ref):
    n1 = z_ref.shape[3]
    c = z_ref.shape[4]
    for j in range(z_ref.shape[1]):
        y = _dot(g_ref[...], z_ref[0, j].reshape(2 * n1, c))
        o_ref[0, j] = (y + vv_ref[0, j] * bias_ref[...]) * x0_ref[0, j]


def _dft_c_call(zt, g, vvp, x0p, bias, cs2):
    nb, n2, _, n1, c = zt.shape
    half = n1 // 2
    full = lambda a: pl.BlockSpec(a.shape, lambda b, i: (0,) * a.ndim)
    tok = pl.BlockSpec((1, cs2, half, c), lambda b, i: (b, i, 0, 0))
    return pl.pallas_call(
        _dft_c_kernel,
        out_shape=jax.ShapeDtypeStruct((nb, n2, half, c), F32),
        grid=(nb, n2 // cs2),
        in_specs=[pl.BlockSpec((1, cs2, 2, n1, c), lambda b, i: (b, i, 0, 0, 0)), full(g), tok, tok, full(bias)],
        out_specs=tok,
        compiler_params=_cparams("arbitrary", "arbitrary"),
        name="dft_stage_c",
    )(zt, g, vvp, x0p, bias)


def _swap_time(x, n2):
    b, l, c = x.shape
    return x.reshape(b, l // n2, n2, c).transpose(0, 2, 1, 3)


def _hyena_two_stage(vv, x0, hfb, inv, bias):
    b, l, c = vv.shape
    n2 = DFT_N2
    n = 2 * l
    n1 = n // n2
    half = n1 // 2
    cs2 = 8
    cf = 8
    c1, s1 = _cis_tables(n1, half, n1)
    f1s = jnp.concatenate([c1, -s1], axis=0).astype(BF16)
    c2, s2 = _cis_tables(n2, n2, n2)
    mf = jnp.concatenate([jnp.concatenate([c2, s2], 1), jnp.concatenate([-s2, c2], 1)], 0).astype(BF16)
    mi = jnp.concatenate([jnp.concatenate([c2, -s2], 1), jnp.concatenate([s2, c2], 1)], 0).astype(BF16)
    gc, gs = _cis_tables(half, n1, n1)
    g = (jnp.concatenate([gc, -gs], axis=1) * (1.0 / n)).astype(BF16)
    twa = [_lane_rep(t) for t in _cis_tables(n2, n1, n)]
    twb = [_lane_rep(t) for t in _cis_tables(n1, n2, n)]
    flip = lambda a: a.transpose(0, 3, 2, 1, 4)
    ka = _dft_a_call(_swap_time(hfb, n2), f1s, twa[0], twa[1], cs2)
    kspec = _spec_call(flip(ka), mf, inv, cf)
    vvp = _swap_time(vv, n2)
    a = _dft_a_call(vvp, f1s, twa[0], twa[1], cs2)
    z = _dft_b_call(flip(a), kspec, mf, mi, twb[0], twb[1], cf)
    yp = _dft_c_call(flip(z), g, vvp, _swap_time(x0, n2), bias, cs2)
    return yp.transpose(0, 2, 1, 3).reshape(b, l, c)


def _dense_conv_kernel(v_ref, x0_ref, h_ref, inv_ref, bias_ref, fd_ref, gi_ref, o_ref):
    n = fd_ref.shape[0] // 2
    fd = fd_ref[...]
    hf = _dot(fd, h_ref[0].astype(BF16))
    hb = _dot(fd, h_ref[1].astype(BF16))
    inv = inv_ref[...]
    kr = (hf[0:n] + hb[0:n]) * inv
    ki = (hf[n:] - hb[n:]) * inv
    v = v_ref[0]
    x = _dot(fd, v.astype(BF16))
    xr, xi = x[0:n], x[n:]
    prod = jnp.concatenate([xr * kr - xi * ki, xr * ki + xi * kr], axis=0).astype(BF16)
    y = _dot(gi_ref[...], prod)
    o_ref[0] = (y + v * bias_ref[...]) * x0_ref[0]


def _hyena_dense(vv, x0, hfb, inv, bias):
    b, l, c = vv.shape
    n = 2 * l
    cf, sf = _cis_tables(n, l, n)
    fd = jnp.concatenate([cf, -sf], axis=0).astype(BF16)
    ci, si = _cis_tables(l, n, n)
    gi = (jnp.concatenate([ci, -si], axis=1) * (1.0 / n)).astype(BF16)
    full = lambda a: pl.BlockSpec(a.shape, lambda bi: (0,) * a.ndim)
    tok = pl.BlockSpec((1, l, c), lambda bi: (bi, 0, 0))
    return pl.pallas_call(
        _dense_conv_kernel,
        out_shape=jax.ShapeDtypeStruct((b, l, c), F32),
        grid=(b,),
        in_specs=[tok, tok, full(hfb), full(inv), full(bias), full(fd), full(gi)],
        out_specs=tok,
        compiler_params=_cparams("arbitrary"),
        name="hyena_dense",
    )(vv, x0, hfb, inv, bias, fd, gi)


def _hyena(vv, x0, p):
    l = vv.shape[1]
    w1p = jnp.zeros((LANES, HY_FFN), F32).at[0:HY_EMB].set(p['hy_w1'])
    row = lambda a: a.reshape(1, -1)
    hfb, norm = _filter_call(w1p, row(p['hy_b1']), p['hy_w2'], row(p['hy_b2']), p['hy_w3'], row(p['hy_freq']),
                             row(p['hy_decay']), l, min(l, 512))
    inv = 1.0 / norm
    bias = row(p['hy_bias'])
    if l >= 8 * DFT_N2:
        return _hyena_two_stage(vv, x0, hfb, inv, bias)
    return _hyena_dense(vv, x0, hfb, inv, bias)


def _post_kernel(att_ref, lru_ref, hy_ref, x_ref, mod_ref, og_ref, wo_ref, fg_ref, rwh_ref, rwl_ref, rb_ref, cnt0_ref,
                 x1_ref, f_ref, sel_ref, gate_ref, rank_ref, cnt_ref, cnt_sc, *, ctx_row):
    row = pl.program_id(0) if ctx_row is None else ctx_row

    @pl.when((pl.program_id(0) == 0) & (pl.program_id(1) == 0))
    def _():
        cnt_sc[...] = cnt0_ref[...]

    d = x_ref.shape[2]
    tm = x_ref.shape[1]
    a_w, l_w = ATT_WIDTH, LRU_WIDTH
    o = (_dot(_rms(att_ref[0], og_ref[:, 0:a_w]).astype(BF16), wo_ref[0:a_w, :])
         + _dot(_rms(lru_ref[0], og_ref[:, a_w:a_w + l_w]).astype(BF16), wo_ref[a_w:a_w + l_w, :])
         + _dot(_rms(hy_ref[0], og_ref[:, a_w + l_w:]).astype(BF16), wo_ref[a_w + l_w:, :]))
    mod = lambda k: mod_ref[pl.ds(row, 1), k * d:(k + 1) * d]
    x1 = x_ref[0] + mod(2) * o
    x1_ref[0] = x1
    f = _rms(x1, fg_ref[...]) * (1.0 + mod(4)) + mod(3)
    f_ref[0] = f
    fh, fl = _split(f)
    logits = _dot(fh, rwh_ref[...]) + (_dot(fl, rwh_ref[...]) + _dot(fh, rwl_ref[...])) + rb_ref[...]
    lane = lax.broadcasted_iota(I32, (tm, LANES), 1)
    sel = jnp.zeros((tm, LANES), I32)
    vals = jnp.zeros((tm, LANES), F32)
    onehots = []
    for k in range(TOP_K):
        m = jnp.max(logits, axis=1, keepdims=True)
        idx = jnp.min(jnp.where(logits == m, lane, LANES), axis=1, keepdims=True)
        hit = lane == idx
        onehots.append(hit.astype(F32))
        sel = jnp.where(lane == k, idx, sel)
        vals = jnp.where(lane == k, m, vals)
        logits = jnp.where(hit, -jnp.inf, logits)
    e = jnp.where(lane < TOP_K, jnp.exp(vals - vals[:, 0:1]), 0.0)
    sel_ref[0] = sel
    gate_ref[0] = e / jnp.sum(e, axis=1, keepdims=True)
    picked = onehots[0] + onehots[1] + onehots[2] + onehots[3]
    earlier = lax.broadcasted_iota(I32, (tm, tm), 0) > lax.broadcasted_iota(I32, (tm, tm), 1)
    before = _dot(earlier.astype(BF16), picked.astype(BF16)) + cnt_sc[...]
    rank = jnp.zeros((tm, LANES), I32)
    for k in range(TOP_K):
        rk = jnp.sum(onehots[k] * before, axis=1, keepdims=True)
        rank = jnp.where(lane == k, rk.astype(I32), rank)
    rank_ref[0] = rank
    cnt_sc[...] += jnp.sum(picked, axis=0, keepdims=True)
    cnt_ref[...] = cnt_sc[...]


def _post_call(att, lru, hyo, x, mod_l, og, wo, fg, rwh, rwl, rb, cnt0, ctx_row, tm):
    b, l, d = x.shape
    full = lambda a: pl.BlockSpec(a.shape, lambda bi, i: (0,) * a.ndim)
    tok = lambda w: pl.BlockSpec((1, tm, w), lambda bi, i: (bi, i, 0))
    consts = [mod_l, og, wo, fg, rwh, rwl, rb, cnt0]
    return pl.pallas_call(
        functools.partial(_post_kernel, ctx_row=ctx_row),
        out_shape=(jax.ShapeDtypeStruct((b, l, d), F32), jax.ShapeDtypeStruct((b, l, d), F32),
                   jax.ShapeDtypeStruct((b, l, LANES), I32), jax.ShapeDtypeStruct((b, l, LANES), F32),
                   jax.ShapeDtypeStruct((b, l, LANES), I32), jax.ShapeDtypeStruct((1, LANES), F32)),
        grid=(b, l // tm),
        in_specs=[tok(ATT_WIDTH), tok(LRU_WIDTH), tok(HY_WIDTH), tok(d)] + [full(a) for a in consts],
        out_specs=(tok(d), tok(d), tok(LANES), tok(LANES), tok(LANES), pl.BlockSpec((1, LANES), lambda bi, i: (0, 0))),
        scratch_shapes=[pltpu.VMEM((1, LANES), F32)],
        compiler_params=_cparams("arbitrary", "arbitrary"),
        name="post_lat" if ctx_row is None else "post_ctx",
    )(att, lru, hyo, x, *consts)


def _route(sel, rank, counts, tm, n_tiles):
    t = sel.shape[0]
    e_ids = jnp.arange(N_EXPERTS, dtype=I32)
    tiles_per = (counts + tm - 1) // tm
    tile_end = jnp.cumsum(tiles_per)
    first_tile = tile_end - tiles_per
    grp_start = jnp.cumsum(counts) - counts
    pos = jnp.sum(jnp.where(sel[:, :, None] == e_ids, first_tile * tm, 0), axis=-1) + rank
    tile_ids = jnp.arange(n_tiles, dtype=I32)
    tile_expert = jnp.minimum(jnp.sum((tile_end[None, :] <= tile_ids[:, None]).astype(I32), axis=1), N_EXPERTS - 1)
    of_tile = lambda v: jnp.sum(jnp.where(tile_expert[:, None] == e_ids, v, 0), axis=1)
    tile_off = jnp.clip(of_tile(grp_start) + (tile_ids - of_tile(first_tile)) * tm, 0, TOP_K * t)
    n_used = tile_end[N_EXPERTS - 1:N_EXPERTS]
    tok = jnp.arange(TOP_K * t, dtype=I32) // TOP_K
    _, tok_sorted = lax.sort((sel.reshape(-1), tok), num_keys=1, is_stable=True)
    n_win = TOP_K * t // IDX_STEP + 2
    padded = jnp.zeros(((n_win + 1) * IDX_STEP,), I32).at[0:TOP_K * t].set(tok_sorted).reshape(n_win + 1, IDX_STEP)
    windows = jnp.concatenate([padded[:-1], padded[1:]], axis=1)
    return pos.astype(I32), tile_expert, tile_off.astype(I32), n_used.astype(I32), windows


def _moe_kernel(te_ref, toff_ref, nu_ref, win_hbm, f_hbm, perm_ref, w1_ref, b1g_ref, b1l_ref, w2_ref, b2_ref, o_ref,
                idx_sm, xbuf, w1g_sc, w1l_sc, w2_sc, sem_idx, sem_row):
    i = pl.program_id(0)
    tm = xbuf.shape[1]
    n_used = nu_ref[0]
    slot = i % 2

    def idx_copy(tile, s):
        win = lax.shift_right_logical(toff_ref[tile], IDX_STEP.bit_length() - 1)
        return pltpu.make_async_copy(win_hbm.at[win], idx_sm.at[s], sem_idx.at[s])

    def start_rows(tile, s):
        delta = toff_ref[tile] & (IDX_STEP - 1)
        for r in range(tm):
            pltpu.make_async_copy(f_hbm.at[pl.ds(idx_sm[s, delta + r], 1), :], xbuf.at[s, pl.ds(r, 1), :],
                                  sem_row.at[s]).start()

    @pl.when(i == 0)
    def _():
        idx_copy(0, 0).start()
        idx_copy(0, 0).wait()
        start_rows(0, 0)

        @pl.when(n_used > 1)
        def _():
            idx_copy(1, 1).start()

    for s in range(2):
        @pl.when((i + 1 < n_used) & (slot == 1 - s))
        def _(s=s):
            idx_copy(i + 1, s).wait()
            start_rows(i + 1, s)

            @pl.when(i + 2 < n_used)
            def _():
                idx_copy(i + 2, 1 - s).start()

    new_expert = (i == 0) | (te_ref[i] != te_ref[jnp.maximum(i - 1, 0)])

    @pl.when(new_expert & (i < n_used))
    def _():
        for jb in range(w1_ref.shape[2] // (2 * LANES)):
            blk = w1_ref[0, :, 2 * jb * LANES:2 * (jb + 1) * LANES].astype(BF16)
            de = _dot(blk, perm_ref[...])
            w1g_sc[:, jb * LANES:(jb + 1) * LANES] = de[:, 0:LANES].astype(BF16)
            w1l_sc[:, jb * LANES:(jb + 1) * LANES] = de[:, LANES:].astype(BF16)
        w2_sc[...] = w2_ref[0].astype(BF16)

    @pl.when(i < n_used)
    def _():
        pltpu.make_async_copy(f_hbm.at[pl.ds(0, tm), :], xbuf.at[slot], sem_row.at[slot]).wait()
        xb = xbuf[slot].astype(BF16)
        glu = jnp.minimum(_dot(xb, w1g_sc[...]) + b1g_ref[0], SWIGLU_LIMIT)
        lin = jnp.clip(_dot(xb, w1l_sc[...]) + b1l_ref[0], -SWIGLU_LIMIT, SWIGLU_LIMIT)
        act = glu * _sigmoid(SWIGLU_ALPHA * glu) * (lin + 1.0)
        o_ref[...] = _dot(act.astype(BF16), w2_sc[...]) + b2_ref[0]

    @pl.when(i >= n_used)
    def _():
        o_ref[...] = jnp.zeros(o_ref.shape, F32)


def _expert_weights(w1, b1, w2, b2):
    return w1, b1[:, None, 0::2], b1[:, None, 1::2], w2, b2[:, None, :]


def _moe_call(tile_expert, tile_off, n_used, windows, f, w1, b1g, b1l, w2, b2, tm):
    n_tiles = tile_expert.shape[0]
    d = f.shape[1]
    dff = w2.shape[1]
    k = jnp.arange(2 * LANES)
    perm = (jnp.where(k % 2 == 0, k // 2, LANES + k // 2)[:, None] == k[None, :]).astype(BF16)
    wspec = lambda shape: pl.BlockSpec((1,) + shape, lambda i, te, toff, nu: (te[i], 0, 0))
    grid_spec = pltpu.PrefetchScalarGridSpec(
        num_scalar_prefetch=3,
        grid=(n_tiles,),
        in_specs=[pl.BlockSpec(memory_space=pl.ANY), pl.BlockSpec(memory_space=pl.ANY),
                  pl.BlockSpec(perm.shape, lambda i, te, toff, nu: (0, 0)),
                  wspec((d, 2 * dff)), wspec((1, dff)), wspec((1, dff)), wspec((dff, d)), wspec((1, d))],
        out_specs=pl.BlockSpec((tm, d), lambda i, te, toff, nu: (i, 0)),
        scratch_shapes=[pltpu.SMEM((2, IDX_WINDOW), I32), pltpu.VMEM((2, tm, d), F32),
                        pltpu.VMEM((d, dff), BF16), pltpu.VMEM((d, dff), BF16), pltpu.VMEM((dff, d), BF16),
                        pltpu.SemaphoreType.DMA((2,)), pltpu.SemaphoreType.DMA((2,))],
    )
    return pl.pallas_call(
        _moe_kernel,
        out_shape=jax.ShapeDtypeStruct((n_tiles * tm, d), F32),
        grid_spec=grid_spec,
        compiler_params=_cparams("arbitrary"),
        name="moe_experts",
    )(tile_expert, tile_off, n_used, windows, f, perm, w1, b1g, b1l, w2, b2)


def _combine_kernel(pos_hbm, y_hbm, gate_ref, x1_ref, mod_ref, o_ref, idx_sm, ybuf, sem_idx, sem_row,
                    *, tiles_per_batch, ctx_row):
    i = pl.program_id(0)
    n = pl.num_programs(0)
    tc = x1_ref.shape[0]
    d = x1_ref.shape[1]
    slot = i % 2

    def idx_copy(tile, s):
        return pltpu.make_async_copy(pos_hbm.at[tile], idx_sm.at[s], sem_idx.at[s])

    def start_rows(s):
        for r in range(TOP_K * tc):
            pltpu.make_async_copy(y_hbm.at[pl.ds(idx_sm[s, r], 1), :], ybuf.at[s, pl.ds(r, 1), :],
                                  sem_row.at[s]).start()

    @pl.when(i == 0)
    def _():
        idx_copy(0, 0).start()
        idx_copy(0, 0).wait()
        start_rows(0)

        @pl.when(n > 1)
        def _():
            idx_copy(1, 1).start()

    for s in range(2):
        @pl.when((i + 1 < n) & (slot == 1 - s))
        def _(s=s):
            idx_copy(i + 1, s).wait()
            start_rows(s)

            @pl.when(i + 2 < n)
            def _():
                idx_copy(i + 2, 1 - s).start()

    pltpu.make_async_copy(y_hbm.at[pl.ds(0, TOP_K * tc), :], ybuf.at[slot], sem_row.at[slot]).wait()
    row = ctx_row if tiles_per_batch is None else i // tiles_per_batch
    gf = mod_ref[pl.ds(row, 1), 5 * d:6 * d]
    acc = None
    for j in range(TOP_K):
        term = gate_ref[:, j:j + 1] * ybuf[slot, j * tc:(j + 1) * tc, :]
        acc = term if acc is None else acc + term
    o_ref[...] = x1_ref[...] + gf * acc


def _combine_call(pos_tiles, y, gate, x1, mod_l, tiles_per_batch, ctx_row, tc):
    t, d = x1.shape
    return pl.pallas_call(
        functools.partial(_combine_kernel, tiles_per_batch=tiles_per_batch, ctx_row=ctx_row),
        out_shape=jax.ShapeDtypeStruct((t, d), F32),
        grid=(t // tc,),
        in_specs=[pl.BlockSpec(memory_space=pl.ANY), pl.BlockSpec(memory_space=pl.ANY),
                  pl.BlockSpec((tc, LANES), lambda i: (i, 0)), pl.BlockSpec((tc, d), lambda i: (i, 0)),
                  pl.BlockSpec(mod_l.shape, lambda i: (0, 0))],
        out_specs=pl.BlockSpec((tc, d), lambda i: (i, 0)),
        scratch_shapes=[pltpu.SMEM((2, TOP_K * tc), I32), pltpu.VMEM((2, TOP_K * tc, d), F32),
                        pltpu.SemaphoreType.DMA((2,)), pltpu.SemaphoreType.DMA((2,))],
        compiler_params=_cparams("arbitrary"),
        name="moe_combine",
    )(pos_tiles, y, gate, x1, mod_l)


def _moe(f, sel, rank, counts, parts, mod_l, ew, ctx_row):
    t, d = f.shape
    tm = MOE_TILE
    tc = 128
    n_tiles = TOP_K * t // tm + N_EXPERTS
    pos, tile_expert, tile_off, n_used, windows = _route(sel[:, 0:TOP_K], rank[:, 0:TOP_K], counts, tm, n_tiles)
    y = _moe_call(tile_expert, tile_off, n_used, windows, f, *ew, tm)
    outs = []
    first = 0
    for x1, gate, seq in parts:
        tp = x1.shape[0]
        pos_tiles = pos[first:first + tp].reshape(tp // tc, tc, TOP_K).transpose(0, 2, 1).reshape(tp // tc, TOP_K * tc)
        outs.append(_combine_call(pos_tiles, y, gate, x1, mod_l, None if seq is None else seq // tc, ctx_row, tc))
        first += tp
    return outs


def _rope_tables(s):
    rows = (jnp.arange(s, dtype=I32) // GRID_W).astype(F32)
    cols = (jnp.arange(s, dtype=I32) % GRID_W).astype(F32)
    nfreq = HEAD_DIM // 4
    inv = ROPE_THETA ** (-jnp.arange(nfreq, dtype=F32) / nfreq)
    ang = jnp.concatenate([rows[:, None] * inv, cols[:, None] * inv], axis=-1)
    cos, sin = jnp.cos(ang), jnp.sin(ang)
    reps = LANES // HEAD_DIM
    return jnp.tile(jnp.concatenate([cos, cos], -1), (1, reps)), jnp.tile(jnp.concatenate([-sin, sin], -1), (1, reps))


def _pick_tk(lk):
    for k in range(768, 0, -LANES):
        if lk % k == 0:
            return k
    raise ValueError(f"{lk} keys do not split into lane-aligned chunks")


def kernel(x, c, ctx, c_ctx, w_mod, b_mod, norm_mix_g, norm_ffn_g, w_in, q_norm_g, k_norm_g, lru_conv_w, lru_conv_b, lru_gate_w, lru_gate_b, lru_lambda, hy_conv_w, hy_conv_b, hy_w1, hy_b1, hy_w2, hy_b2, hy_w3, hy_freq, hy_decay, hy_bias, out_norm_g, w_out, router_w, router_b, moe_w1, moe_b1, moe_w2, moe_b2):
    b, s, d = x.shape
    lc = ctx.shape[1]
    depth = w_mod.shape[0]
    ctx_row = b
    assert b < SUBLANES and d == D_MODEL
    c8 = jnp.concatenate([c, c_ctx[None, :], jnp.zeros((SUBLANES - b - 1, d), F32)], axis=0)
    mods = _mod_call(c8, w_mod, b_mod)
    rope_tabs = _rope_tables(s)
    lane = jnp.arange(LANES)
    bd = ((lane[:, None] // HEAD_DIM) == (lane[None, :] // HEAD_DIM)).astype(F32) * (1.0 / HEAD_DIM)
    bd = bd.astype(BF16)
    row = lambda a: a.reshape(1, -1)
    tile_heads = lambda g: jnp.tile(g.reshape(1, HEAD_DIM), (1, LANES // HEAD_DIM))
    tm = min(512, s)
    tmc = min(256, lc)
    t_lat = b * s

    for l in range(depth):
        last = l == depth - 1
        mod_l = mods[l]
        w_in_l = w_in[l].astype(BF16)
        qg, kg = tile_heads(q_norm_g[l]), tile_heads(k_norm_g[l])
        p = {'hy_w1': hy_w1[l], 'hy_b1': hy_b1[l], 'hy_w2': hy_w2[l], 'hy_b2': hy_b2[l], 'hy_w3': hy_w3[l],
             'hy_freq': hy_freq[l], 'hy_decay': hy_decay[l], 'hy_bias': hy_bias[l]}
        conv = (lru_conv_w[l], row(lru_conv_b[l]), hy_conv_w[l], row(hy_conv_b[l]))

        q, k, v, rest = _in_proj_call(x, mod_l, row(norm_mix_g[l]), w_in_l, qg, kg, bd, rope_tabs, None, tm)
        qc, kc, vc, rest_c = _in_proj_call(ctx, mod_l, row(norm_mix_g[l]), w_in_l, qg, kg, bd, None, ctx_row, tmc)
        xl, vv, x0 = _prep_call(rest, *conv, tm)
        xl_c, vv_c, x0_c = _prep_call(rest_c, *conv, tmc)

        kd = _dup_heads(jnp.concatenate([kc, k], axis=1))
        vd = _val_heads(jnp.concatenate([vc, v], axis=1))
        att = _attn_call(q, kd, vd, min(256, s), _pick_tk(lc + s))

        zeros_h = jnp.zeros((b, 1, LRU_WIDTH), F32)
        gws = [_lru_gate_weights(lru_gate_w[l, dr], lru_gate_b[l, dr]) for dr in range(2)]
        lams = [row(lru_lambda[l, dr]) for dr in range(2)]
        hc_f = _lru_call(xl_c, *gws[0], lams[0], zeros_h, False, tmc)
        hc_b, lru_c = _lru_call(xl_c, *gws[1], lams[1], zeros_h, True, tmc, hf=hc_f, rest=rest_c)
        h_f = _lru_call(xl, *gws[0], lams[0], hc_f[:, lc - 1:lc], False, tm)
        _, lru = _lru_call(xl, *gws[1], lams[1], hc_b[:, 0:1], True, tm, hf=h_f, rest=rest)

        hyo = _hyena(vv, x0, p)

        rw = jnp.zeros((d, LANES), F32).at[:, 0:N_EXPERTS].set(router_w[l])
        rwh, rwl = _split(rw)
        rb = jnp.full((1, LANES), -1e30, F32).at[0, 0:N_EXPERTS].set(router_b[l])
        post_consts = (row(out_norm_g[l]), w_out[l].astype(BF16), row(norm_ffn_g[l]), rwh, rwl, rb)
        no_picks = jnp.zeros((1, LANES), F32)
        x1, f, sel, gate, rank, cnt = _post_call(att, lru, hyo, x, mod_l, *post_consts, no_picks, None, tm)
        ew = _expert_weights(moe_w1[l], moe_b1[l], moe_w2[l], moe_b2[l])
        flat = lambda a: a.reshape(-1, a.shape[-1])
        counts = lambda cn: cn[0, 0:N_EXPERTS].astype(I32)
        if last:
            (x,) = _moe(flat(f), flat(sel), flat(rank), counts(cnt), [(flat(x1), flat(gate), s)], mod_l, ew, ctx_row)
            x = x.reshape(b, s, d)
        else:
            att_c = _attn_call(qc, _dup_heads(kc), _val_heads(vc), tmc, _pick_tk(lc))
            hyo_c = _hyena(vv_c, x0_c, p)
            x1c, fc, selc, gatec, rankc, cnt = _post_call(att_c, lru_c, hyo_c, ctx, mod_l, *post_consts, cnt, ctx_row,
                                                         tmc)
            cat = lambda a, bb: jnp.concatenate([flat(a), flat(bb)], axis=0)
            parts = [(flat(x1), flat(gate), s), (flat(x1c), flat(gatec), None)]
            x, ctx = _moe(cat(f, fc), cat(sel, selc), cat(rank, rankc), counts(cnt), parts, mod_l, ew, ctx_row)
            x = x.reshape(b, s, d)
            ctx = ctx.reshape(b, lc, d)
    return x
```

```python
import functools
import math

import jax
import jax.numpy as jnp
from jax import lax
from jax.experimental import pallas as pl
from jax.experimental.pallas import tpu as pltpu

F32 = jnp.float32
BF16 = jnp.bfloat16
I32 = jnp.int32

D_MODEL = 1024
HEAD_DIM = 64
ATT_WIDTH = 512
KV_WIDTH = 128
LRU_WIDTH = 256
HY_WIDTH = 256
REST_WIDTH = 2 * LRU_WIDTH + 3 * HY_WIDTH
IN_WIDTH = ATT_WIDTH + 2 * KV_WIDTH + REST_WIDTH
GRID_W = 64
ROPE_THETA = 10000.0
LRU_C = 8.0
HY_EMB = 33
HY_BANDS = 16
HY_FFN = 64
N_EXPERTS = 32
TOP_K = 4
D_FF = 1024
SWIGLU_ALPHA = 1.702
SWIGLU_LIMIT = 7.0
EPS = 1e-6
LANES = 128
SUBLANES = 8
DFT_N2 = 128
VMEM_LIMIT = 56 * 1024 * 1024
MOE_TILE = 256
IDX_STEP = 1024
IDX_WINDOW = 2 * IDX_STEP
Q_SCALE = HEAD_DIM ** -0.5 * math.log2(math.e)


def _cparams(*sem):
    return pltpu.CompilerParams(dimension_semantics=sem, vmem_limit_bytes=VMEM_LIMIT)


def _dot(a, b):
    return jnp.dot(a, b, preferred_element_type=F32)


def _split(x):
    hi = x.astype(BF16)
    return hi, (x - hi.astype(F32)).astype(BF16)


def _dot3(a, b):
    ah, al = _split(a)
    bh, bl = _split(b)
    return _dot(ah, bh) + (_dot(al, bh) + _dot(ah, bl))


def _sigmoid(x):
    return 1.0 / (1.0 + jnp.exp(-x))


def _rms(x, g):
    return x * lax.rsqrt(jnp.mean(x * x, axis=-1, keepdims=True) + EPS) * g


def _mod_kernel(c_ref, w_ref, b_ref, o_ref):
    c = c_ref[...]
    o_ref[0] = _dot3(c * _sigmoid(c), w_ref[0]) + b_ref[0]


def _mod_call(c8, w_mod, b_mod):
    depth, d, n6 = w_mod.shape
    tn = n6 // 4
    return pl.pallas_call(
        _mod_kernel,
        out_shape=jax.ShapeDtypeStruct((depth, SUBLANES, n6), F32),
        grid=(depth, n6 // tn),
        in_specs=[pl.BlockSpec((SUBLANES, d), lambda l, j: (0, 0)),
                  pl.BlockSpec((1, d, tn), lambda l, j: (l, 0, j)),
                  pl.BlockSpec((1, 1, tn), lambda l, j: (l, 0, j))],
        out_specs=pl.BlockSpec((1, SUBLANES, tn), lambda l, j: (l, 0, j)),
        compiler_params=_cparams("arbitrary", "arbitrary"),
        name="mod",
    )(c8, w_mod, b_mod.reshape(depth, 1, n6))


def _head_norm(x, g, bd):
    hi, lo = _split(x * x)
    ms = _dot(hi, bd) + _dot(lo, bd)
    return x * lax.rsqrt(ms + EPS) * g


def _rope(y, cos, sin, first_half):
    swapped = jnp.where(first_half, pltpu.roll(y, LANES - HEAD_DIM // 2, 1), pltpu.roll(y, HEAD_DIM // 2, 1))
    return y * cos + swapped * sin


def _in_proj_kernel(*refs, rope, ctx_row):
    if rope:
        x_ref, mod_ref, g_ref, w_ref, qg_ref, kg_ref, bd_ref, cos_ref, sin_ref, q_ref, k_ref, v_ref, r_ref = refs
    else:
        x_ref, mod_ref, g_ref, w_ref, qg_ref, kg_ref, bd_ref, q_ref, k_ref, v_ref, r_ref = refs
    row = pl.program_id(0) if ctx_row is None else ctx_row
    d = x_ref.shape[2]
    tm = x_ref.shape[1]
    shift = mod_ref[pl.ds(row, 1), 0:d]
    scale = mod_ref[pl.ds(row, 1), d:2 * d]
    h = _rms(x_ref[0], g_ref[...]) * (1.0 + scale) + shift
    u = _dot(h.astype(BF16), w_ref[...])
    bd = bd_ref[...]
    if rope:
        cos = cos_ref[...]
        sin = sin_ref[...]
        lane = lax.broadcasted_iota(I32, (tm, LANES), 1)
        first_half = (lane & (HEAD_DIM // 2)) == 0
    for p in range(ATT_WIDTH // LANES):
        y = _head_norm(u[:, p * LANES:(p + 1) * LANES], qg_ref[...], bd)
        if rope:
            y = _rope(y, cos, sin, first_half)
        q_ref[0, :, p * LANES:(p + 1) * LANES] = (y * Q_SCALE).astype(BF16)
    y = _head_norm(u[:, ATT_WIDTH:ATT_WIDTH + KV_WIDTH], kg_ref[...], bd)
    if rope:
        y = _rope(y, cos, sin, first_half)
    k_ref[0] = y.astype(BF16)
    v_ref[0] = u[:, ATT_WIDTH + KV_WIDTH:ATT_WIDTH + 2 * KV_WIDTH].astype(BF16)
    r_ref[0] = u[:, ATT_WIDTH + 2 * KV_WIDTH:]


def _in_proj_call(x, mod_l, g, w_in, qg, kg, bd, rope_tabs, ctx_row, tm):
    b, l, d = x.shape
    rope = rope_tabs is not None
    full = lambda shape: pl.BlockSpec(shape, lambda bi, i: (0,) * len(shape))
    in_specs = [pl.BlockSpec((1, tm, d), lambda bi, i: (bi, i, 0)),
                full(mod_l.shape), full((1, d)), full(w_in.shape), full((1, LANES)), full((1, LANES)),
                full((LANES, LANES))]
    args = [x, mod_l, g, w_in, qg, kg, bd]
    if rope:
        in_specs += [pl.BlockSpec((tm, LANES), lambda bi, i: (i, 0))] * 2
        args += list(rope_tabs)
    tok = lambda w: pl.BlockSpec((1, tm, w), lambda bi, i: (bi, i, 0))
    return pl.pallas_call(
        functools.partial(_in_proj_kernel, rope=rope, ctx_row=ctx_row),
        out_shape=(jax.ShapeDtypeStruct((b, l, ATT_WIDTH), BF16), jax.ShapeDtypeStruct((b, l, KV_WIDTH), BF16),
                   jax.ShapeDtypeStruct((b, l, KV_WIDTH), BF16), jax.ShapeDtypeStruct((b, l, REST_WIDTH), F32)),
        grid=(b, l // tm),
        in_specs=in_specs,
        out_specs=(tok(ATT_WIDTH), tok(KV_WIDTH), tok(KV_WIDTH), tok(REST_WIDTH)),
        compiler_params=_cparams("arbitrary", "arbitrary"),
        name="in_proj_lat" if rope else "in_proj_ctx",
    )(*args)


def _conv_taps(win, w, left, tm):
    acc = None
    for j in range(w.shape[0]):
        off = SUBLANES + j - left
        term = win[off:off + tm] * w[j:j + 1]
        acc = term if acc is None else acc + term
    return acc


def _prep_kernel(cur_ref, prev_ref, next_ref, lw_ref, lb_ref, hw_ref, hb_ref, xl_ref, vv_ref, x0_ref):
    i = pl.program_id(1)
    tm = cur_ref.shape[1]
    has_prev = (i > 0).astype(F32)
    has_next = (i < pl.num_programs(1) - 1).astype(F32)

    def window(lo, hi):
        return jnp.concatenate([prev_ref[0, :, lo:hi] * has_prev, cur_ref[0, :, lo:hi],
                                next_ref[0, :, lo:hi] * has_next], axis=0)

    xl_ref[0] = _conv_taps(window(0, LRU_WIDTH), lw_ref[...], 2, tm) + lb_ref[...]
    uc = _conv_taps(window(2 * LRU_WIDTH, REST_WIDTH), hw_ref[...], 1, tm) + hb_ref[...]
    x0_ref[0] = uc[:, 0:HY_WIDTH]
    vv_ref[0] = uc[:, 2 * HY_WIDTH:] * uc[:, HY_WIDTH:2 * HY_WIDTH]


def _prep_call(rest, lw, lb, hw, hb, tm):
    b, l, w = rest.shape
    nb = tm // SUBLANES
    last = l // SUBLANES - 1
    full = lambda a: pl.BlockSpec(a.shape, lambda bi, i: (0,) * a.ndim)
    tok = pl.BlockSpec((1, tm, LRU_WIDTH), lambda bi, i: (bi, i, 0))
    out = jax.ShapeDtypeStruct((b, l, LRU_WIDTH), F32)
    return pl.pallas_call(
        _prep_kernel,
        out_shape=(out, out, out),
        grid=(b, l // tm),
        in_specs=[pl.BlockSpec((1, tm, w), lambda bi, i: (bi, i, 0)),
                  pl.BlockSpec((1, SUBLANES, w), lambda bi, i: (bi, jnp.maximum(i * nb - 1, 0), 0)),
                  pl.BlockSpec((1, SUBLANES, w), lambda bi, i: (bi, jnp.minimum((i + 1) * nb, last), 0)),
                  full(lw), full(lb), full(hw), full(hb)],
        out_specs=(tok, tok, tok),
        compiler_params=_cparams("arbitrary", "arbitrary"),
        name="conv_prep",
    )(rest, rest, rest, lw, lb, hw, hb)


def _attn_kernel(q_ref, k_ref, v_ref, o_ref, qst_sc, m_sc, acc_sc, pa_sc, pb_sc, aa_sc, ab_sc, *, tk):
    tq = q_ref.shape[1]
    lk = k_ref.shape[2]
    nl = tk // LANES
    low = lax.broadcasted_iota(I32, (tq, LANES), 1) < HEAD_DIM
    for g in range(2):
        for n, p in enumerate((2 * g, 2 * g + 1)):
            qs = q_ref[0, :, p * LANES:(p + 1) * LANES]
            zero = jnp.zeros_like(qs)
            qst_sc[g, 2 * n * tq:(2 * n + 1) * tq, :] = jnp.where(low, qs, zero)
            qst_sc[g, (2 * n + 1) * tq:(2 * n + 2) * tq, :] = jnp.where(low, zero, qs)
    m_sc[...] = jnp.full(m_sc.shape, -jnp.inf, F32)
    acc_sc[...] = jnp.zeros(acc_sc.shape, F32)

    def scores(c, p_sc, a_sc):
        start = pl.multiple_of(c * tk, tk)
        for g in range(2):
            kc = k_ref[0, g, pl.ds(start, tk), :]
            s = lax.dot_general(qst_sc[g], kc, (((1,), (1,)), ((), ())), preferred_element_type=F32)
            cols = [s[:, n * LANES:(n + 1) * LANES] for n in range(nl)]
            mx = cols[0]
            for col in cols[1:]:
                mx = jnp.maximum(mx, col)
            m_old = m_sc[g]
            m_new = jnp.maximum(m_old, jnp.max(mx, axis=1, keepdims=True))
            a_sc[g] = jnp.exp2(m_old - m_new)
            p_sc[g] = jnp.concatenate([jnp.exp2(col - m_new).astype(BF16) for col in cols], axis=1)
            m_sc[g] = m_new

    def values(c, p_sc, a_sc):
        start = pl.multiple_of(c * tk, tk)
        for g in range(2):
            acc_sc[g] = a_sc[g] * acc_sc[g] + _dot(p_sc[g], v_ref[0, g, pl.ds(start, tk), :])

    nk = lk // tk
    scores(0, pa_sc, aa_sc)

    def body(i, carry):
        scores(2 * i + 1, pb_sc, ab_sc)
        values(2 * i, pa_sc, aa_sc)
        scores(2 * i + 2, pa_sc, aa_sc)
        values(2 * i + 1, pb_sc, ab_sc)
        return carry

    lax.fori_loop(0, (nk - 1) // 2, body, 0)
    if nk % 2:
        values(nk - 1, pa_sc, aa_sc)
    else:
        scores(nk - 1, pb_sc, ab_sc)
        values(nk - 2, pa_sc, aa_sc)
        values(nk - 1, pb_sc, ab_sc)
    for g in range(2):
        acc = acc_sc[g]
        o = acc / acc[:, HEAD_DIM:HEAD_DIM + 1]
        for n, p in enumerate((2 * g, 2 * g + 1)):
            o_ref[0, :, p * LANES:(p + 1) * LANES] = jnp.where(
                low, o[2 * n * tq:(2 * n + 1) * tq], pltpu.roll(o[(2 * n + 1) * tq:(2 * n + 2) * tq], HEAD_DIM, 1))


def _attn_call(q, kd, vd, tq, tk):
    b, l, _ = q.shape
    lk = kd.shape[2]
    kv = pl.BlockSpec((1, 2, lk, LANES), lambda bi, i: (bi, 0, 0, 0))
    return pl.pallas_call(
        functools.partial(_attn_kernel, tk=tk),
        out_shape=jax.ShapeDtypeStruct((b, l, ATT_WIDTH), F32),
        grid=(b, l // tq),
        in_specs=[pl.BlockSpec((1, tq, ATT_WIDTH), lambda bi, i: (bi, i, 0)), kv, kv],
        out_specs=pl.BlockSpec((1, tq, ATT_WIDTH), lambda bi, i: (bi, i, 0)),
        scratch_shapes=[pltpu.VMEM((2, 4 * tq, LANES), BF16), pltpu.VMEM((2, 4 * tq, LANES), F32),
                        pltpu.VMEM((2, 4 * tq, LANES), F32),
                        pltpu.VMEM((2, 4 * tq, tk), BF16), pltpu.VMEM((2, 4 * tq, tk), BF16),
                        pltpu.VMEM((2, 4 * tq, LANES), F32), pltpu.VMEM((2, 4 * tq, LANES), F32)],
        compiler_params=_cparams("arbitrary", "arbitrary"),
        name="attn",
    )(q, kd, vd)


def _dup_heads(x):
    halves = [jnp.concatenate([x[..., g * HEAD_DIM:(g + 1) * HEAD_DIM]] * 2, axis=-1) for g in range(2)]
    return jnp.stack(halves, axis=1)


def _val_heads(x):
    pad = jnp.zeros(x.shape[:-1] + (LANES - HEAD_DIM,), x.dtype).at[..., 0].set(1)
    return jnp.stack([jnp.concatenate([x[..., g * HEAD_DIM:(g + 1) * HEAD_DIM], pad], axis=-1) for g in range(2)], 1)


def _gelu_tanh(x):
    return 0.5 * x * (1.0 + jnp.tanh(math.sqrt(2.0 / math.pi) * (x + 0.044715 * (x * x * x))))


def _lru_kernel(*refs, reverse):
    if reverse:
        xl_ref, w_ref, b_ref, lam_ref, h0_ref, hf_ref, lg_ref, h_ref, o_ref, a_sc, b_sc, c_sc = refs
    else:
        xl_ref, w_ref, b_ref, lam_ref, h0_ref, h_ref, a_sc, b_sc, c_sc = refs
    tc = xl_ref.shape[1]
    nt = tc // SUBLANES

    @pl.when(pl.program_id(1) == 0)
    def _():
        c_sc[...] = jnp.broadcast_to(h0_ref[0], c_sc.shape)

    xc = xl_ref[0]
    gates = _dot(xc.astype(BF16), w_ref[...]) + b_ref[...]
    r = _sigmoid(gates[:, 0:LRU_WIDTH])
    gi = _sigmoid(gates[:, LRU_WIDTH:])
    nlam = -lam_ref[...]
    softplus = jnp.maximum(nlam, 0.0) + jnp.log1p(jnp.exp(-jnp.abs(nlam)))
    log_a = -LRU_C * r * softplus
    a = jnp.exp(log_a)
    a_sc[...] = a
    b_sc[...] = jnp.sqrt(1.0 - a * a) * (gi * xc)
    row = lax.broadcasted_iota(I32, (SUBLANES, LRU_WIDTH), 0)

    def tile(k, carry):
        kk = nt - 1 - k if reverse else k
        start = pl.multiple_of(kk * SUBLANES, SUBLANES)
        a = a_sc[pl.ds(start, SUBLANES), :]
        bv = b_sc[pl.ds(start, SUBLANES), :]
        for dist in (1, 2, 4):
            shift = SUBLANES - dist if reverse else dist
            keep = (row < SUBLANES - dist) if reverse else (row >= dist)
            a_n = pltpu.roll(a, shift, 0)
            b_n = pltpu.roll(bv, shift, 0)
            bv = jnp.where(keep, a * b_n + bv, bv)
            a = jnp.where(keep, a * a_n, a)
        h = a * c_sc[...] + bv
        h_ref[0, pl.ds(start, SUBLANES), :] = h
        edge = h[0:1] if reverse else h[SUBLANES - 1:SUBLANES]
        c_sc[...] = jnp.broadcast_to(edge, c_sc.shape)
        return carry

    lax.fori_loop(0, nt, tile, 0)
    if reverse:
        o_ref[0] = (hf_ref[0] + h_ref[0]) * _gelu_tanh(lg_ref[0])


def _lru_call(xl, w, bias, lam, h0, reverse, tc, hf=None, rest=None):
    b, l, c = xl.shape
    nc = l // tc
    chunk = (lambda bi, i: (bi, nc - 1 - i, 0)) if reverse else (lambda bi, i: (bi, i, 0))
    full = lambda a: pl.BlockSpec(a.shape, lambda bi, i: (0,) * a.ndim)
    tok = pl.BlockSpec((1, tc, c), chunk)
    in_specs = [tok, full(w), full(bias), full(lam), pl.BlockSpec((1, 1, c), lambda bi, i: (bi, 0, 0))]
    args = [xl, w, bias, lam, h0]
    out = jax.ShapeDtypeStruct((b, l, c), F32)
    if reverse:
        lg_chunk = (lambda bi, i: (bi, nc - 1 - i, 1))
        in_specs += [tok, pl.BlockSpec((1, tc, c), lg_chunk)]
        args += [hf, rest]
        out_shape, out_specs = (out, out), (tok, tok)
    else:
        out_shape, out_specs = out, tok
    return pl.pallas_call(
        functools.partial(_lru_kernel, reverse=reverse),
        out_shape=out_shape,
        grid=(b, nc),
        in_specs=in_specs,
        out_specs=out_specs,
        scratch_shapes=[pltpu.VMEM((tc, c), F32), pltpu.VMEM((tc, c), F32), pltpu.VMEM((SUBLANES, c), F32)],
        compiler_params=_cparams("arbitrary", "arbitrary"),
        name="lru_bwd" if reverse else "lru_fwd",
    )(*args)


def _lru_gate_weights(gw, gb):
    nb, bw = gw.shape[1], gw.shape[2]
    cols = []
    for j in range(2):
        m = jnp.zeros((nb * bw, nb * bw), F32)
        for n in range(nb):
            m = m.at[n * bw:(n + 1) * bw, n * bw:(n + 1) * bw].set(gw[j, n])
        cols.append(m)
    return jnp.concatenate(cols, axis=1).astype(BF16), gb.reshape(1, -1)


def _filter_kernel(w1_ref, b1_ref, w2_ref, b2_ref, w3_ref, fr_ref, dec_ref, h_ref, n_ref, *, seq):
    i = pl.program_id(0)
    tl = h_ref.shape[1]
    c = h_ref.shape[2]
    t = (i * tl + lax.broadcasted_iota(I32, (tl, LANES), 0)).astype(F32)
    lane = lax.broadcasted_iota(I32, (tl, LANES), 1)
    t01 = t * (1.0 / (seq - 1))
    kidx = jnp.where(lane <= HY_BANDS, lane - 1, lane - 1 - HY_BANDS).astype(F32)
    band = 1e-4 + kidx * ((HY_BANDS - 1 - 1e-4) / (HY_BANDS - 1))
    ang = (2.0 * math.pi / seq) * t * band
    z = jnp.where(lane == 0, t01,
                  jnp.where(lane <= HY_BANDS, jnp.cos(ang),
                            jnp.where(lane <= 2 * HY_BANDS, -jnp.sin(ang), 0.0)))
    fr = fr_ref[...]
    h = jnp.sin(fr * (_dot3(z, w1_ref[...]) + b1_ref[...]))
    h = jnp.sin(fr * (_dot3(h, w2_ref[...]) + b2_ref[...]))
    h = _dot3(h, w3_ref[...]) * jnp.exp(-t01[:, 0:1] * jnp.abs(dec_ref[...]))
    hf = h[:, 0:c]
    hb = jnp.where(t[:, 0:1] == 0.0, 0.0, h[:, c:])
    h_ref[0] = hf
    h_ref[1] = hb

    @pl.when(i == 0)
    def _():
        n_ref[...] = jnp.zeros(n_ref.shape, F32)

    n_ref[...] += jnp.sum(jnp.abs(hf) + jnp.abs(hb), axis=0, keepdims=True)


def _filter_call(w1p, b1, w2, b2, w3, fr, dec, seq, tl):
    c = w3.shape[1] // 2
    full = lambda a: pl.BlockSpec(a.shape, lambda i: (0,) * a.ndim)
    args = [w1p, b1, w2, b2, w3, fr, dec]
    return pl.pallas_call(
        functools.partial(_filter_kernel, seq=seq),
        out_shape=(jax.ShapeDtypeStruct((2, seq, c), F32), jax.ShapeDtypeStruct((1, c), F32)),
        grid=(seq // tl,),
        in_specs=[full(a) for a in args],
        out_specs=(pl.BlockSpec((2, tl, c), lambda i: (0, i, 0)), pl.BlockSpec((1, c), lambda i: (0, 0))),
        compiler_params=_cparams("arbitrary"),
        name="hyena_filter",
    )(*args)


def _cis_tables(na, nb, n):
    prod = (jnp.arange(na, dtype=I32)[:, None] * jnp.arange(nb, dtype=I32)[None, :]) % n
    ang = prod.astype(F32) * (2.0 * math.pi / n)
    return jnp.cos(ang), jnp.sin(ang)


def _lane_rep(x):
    return jnp.broadcast_to(x[:, :, None], x.shape + (LANES,))


def _dft_a_kernel(v_ref, f_ref, tc_ref, ts_ref, o_ref):
    n1 = tc_ref.shape[1]
    c = v_ref.shape[3]
    rep = c // LANES
    for j in range(v_ref.shape[1]):
        p_ = _dot(f_ref[...], v_ref[0, j].astype(BF16))
        pr, pi = p_[0:n1], p_[n1:]
        tc = jnp.concatenate([tc_ref[j]] * rep, axis=1)
        ts = jnp.concatenate([ts_ref[j]] * rep, axis=1)
        o_ref[0, j, 0] = (pr * tc + pi * ts).astype(BF16)
        o_ref[0, j, 1] = (pi * tc - pr * ts).astype(BF16)


def _dft_a_call(vp, f1s, tc, ts, cs2):
    nb, n2, half, c = vp.shape
    n1 = 2 * half
    full = lambda a: pl.BlockSpec(a.shape, lambda b, i: (0,) * a.ndim)
    return pl.pallas_call(
        _dft_a_kernel,
        out_shape=jax.ShapeDtypeStruct((nb, n2, 2, n1, c), BF16),
        grid=(nb, n2 // cs2),
        in_specs=[pl.BlockSpec((1, cs2, half, c), lambda b, i: (b, i, 0, 0)), full(f1s),
                  pl.BlockSpec((cs2, n1, LANES), lambda b, i: (i, 0, 0)),
                  pl.BlockSpec((cs2, n1, LANES), lambda b, i: (i, 0, 0))],
        out_specs=pl.BlockSpec((1, cs2, 2, n1, c), lambda b, i: (b, i, 0, 0, 0)),
        compiler_params=_cparams("arbitrary", "arbitrary"),
        name="dft_stage_a",
    )(vp, f1s, tc, ts)


def _spec_kernel(a_ref, mf_ref, inv_ref, o_ref):
    n2 = a_ref.shape[3]
    c = a_ref.shape[4]
    inv = inv_ref[...]
    for j in range(a_ref.shape[1]):
        yf = _dot(mf_ref[...], a_ref[0, j].reshape(2 * n2, c))
        yb = _dot(mf_ref[...], a_ref[1, j].reshape(2 * n2, c))
        o_ref[j, 0] = (yf[0:n2] + yb[0:n2]) * inv
        o_ref[j, 1] = (yf[n2:] - yb[n2:]) * inv


def _spec_call(at, mf, inv, cf):
    _, n1, _, n2, c = at.shape
    full = lambda a: pl.BlockSpec(a.shape, lambda i: (0,) * a.ndim)
    return pl.pallas_call(
        _spec_kernel,
        out_shape=jax.ShapeDtypeStruct((n1, 2, n2, c), F32),
        grid=(n1 // cf,),
        in_specs=[pl.BlockSpec((2, cf, 2, n2, c), lambda i: (0, i, 0, 0, 0)), full(mf), full(inv)],
        out_specs=pl.BlockSpec((cf, 2, n2, c), lambda i: (i, 0, 0, 0)),
        compiler_params=_cparams("arbitrary"),
        name="hyena_spectrum",
    )(at, mf, inv)


def _dft_b_kernel(a_ref, k_ref, mf_ref, mi_ref, tc_ref, ts_ref, o_ref):
    n2 = a_ref.shape[3]
    c = a_ref.shape[4]
    rep = c // LANES
    for j in range(a_ref.shape[1]):
        y = _dot(mf_ref[...], a_ref[0, j].reshape(2 * n2, c))
        yr, yi = y[0:n2], y[n2:]
        kr, ki = k_ref[j, 0], k_ref[j, 1]
        prod = jnp.concatenate([yr * kr - yi * ki, yr * ki + yi * kr], axis=0).astype(BF16)
        z = _dot(mi_ref[...], prod)
        zr, zi = z[0:n2], z[n2:]
        tc = jnp.concatenate([tc_ref[j]] * rep, axis=1)
        ts = jnp.concatenate([ts_ref[j]] * rep, axis=1)
        o_ref[0, j, 0] = (zr * tc - zi * ts).astype(BF16)
        o_ref[0, j, 1] = (zr * ts + zi * tc).astype(BF16)


def _dft_b_call(at, kspec, mf, mi, tc, ts, cf):
    nb, n1, _, n2, c = at.shape
    full = lambda a: pl.BlockSpec(a.shape, lambda b, i: (0,) * a.ndim)
    blk = pl.BlockSpec((1, cf, 2, n2, c), lambda b, i: (b, i, 0, 0, 0))
    tw = pl.BlockSpec((cf, n2, LANES), lambda b, i: (i, 0, 0))
    return pl.pallas_call(
        _dft_b_kernel,
        out_shape=jax.ShapeDtypeStruct((nb, n1, 2, n2, c), BF16),
        grid=(nb, n1 // cf),
        in_specs=[blk, pl.BlockSpec((cf, 2, n2, c), lambda b, i: (i, 0, 0, 0)), full(mf), full(mi), tw, tw],
        out_specs=blk,
        compiler_params=_cparams("arbitrary", "arbitrary"),
        name="dft_stage_b",
    )(at, kspec, mf, mi, tc, ts)


def _dft_c_kernel(z_ref, g_ref, vv_ref, x0_ref, bias_ref, o_ref):
    n1 = z_ref.shape[3]
    c = z_ref.shape[4]
    for j in range(z_ref.shape[1]):
        y = _dot(g_ref[...], z_ref[0, j].reshape(2 * n1, c))
        o_ref[0, j] = (y + vv_ref[0, j] * bias_ref[...]) * x0_ref[0, j]


def _dft_c_call(zt, g, vvp, x0p, bias, cs2):
    nb, n2, _, n1, c = zt.shape
    half = n1 // 2
    full = lambda a: pl.BlockSpec(a.shape, lambda b, i: (0,) * a.ndim)
    tok = pl.BlockSpec((1, cs2, half, c), lambda b, i: (b, i, 0, 0))
    return pl.pallas_call(
        _dft_c_kernel,
        out_shape=jax.ShapeDtypeStruct((nb, n2, half, c), F32),
        grid=(nb, n2 // cs2),
        in_specs=[pl.BlockSpec((1, cs2, 2, n1, c), lambda b, i: (b, i, 0, 0, 0)), full(g), tok, tok, full(bias)],
        out_specs=tok,
        compiler_params=_cparams("arbitrary", "arbitrary"),
        name="dft_stage_c",
    )(zt, g, vvp, x0p, bias)


def _swap_time(x, n2):
    b, l, c = x.shape
    return x.reshape(b, l // n2, n2, c).transpose(0, 2, 1, 3)


def _hyena_two_stage(vv, x0, hfb, inv, bias):
    b, l, c = vv.shape
    n2 = DFT_N2
    n = 2 * l
    n1 = n // n2
    half = n1 // 2
    cs2 = 8
    cf = 8
    c1, s1 = _cis_tables(n1, half, n1)
    f1s = jnp.concatenate([c1, -s1], axis=0).astype(BF16)
    c2, s2 = _cis_tables(n2, n2, n2)
    mf = jnp.concatenate([jnp.concatenate([c2, s2], 1), jnp.concatenate([-s2, c2], 1)], 0).astype(BF16)
    mi = jnp.concatenate([jnp.concatenate([c2, -s2], 1), jnp.concatenate([s2, c2], 1)], 0).astype(BF16)
    gc, gs = _cis_tables(half, n1, n1)
    g = (jnp.concatenate([gc, -gs], axis=1) * (1.0 / n)).astype(BF16)
    twa = [_lane_rep(t) for t in _cis_tables(n2, n1, n)]
    twb = [_lane_rep(t) for t in _cis_tables(n1, n2, n)]
    flip = lambda a: a.transpose(0, 3, 2, 1, 4)
    ka = _dft_a_call(_swap_time(hfb, n2), f1s, twa[0], twa[1], cs2)
    kspec = _spec_call(flip(ka), mf, inv, cf)
    vvp = _swap_time(vv, n2)
    a = _dft_a_call(vvp, f1s, twa[0], twa[1], cs2)
    z = _dft_b_call(flip(a), kspec, mf, mi, twb[0], twb[1], cf)
    yp = _dft_c_call(flip(z), g, vvp, _swap_time(x0, n2), bias, cs2)
    return yp.transpose(0, 2, 1, 3).reshape(b, l, c)


def _dense_conv_kernel(v_ref, x0_ref, h_ref, inv_ref, bias_ref, fd_ref, gi_ref, o_ref):
    n = fd_ref.shape[0] // 2
    fd = fd_ref[...]
    hf = _dot(fd, h_ref[0].astype(BF16))
    hb = _dot(fd, h_ref[1].astype(BF16))
    inv = inv_ref[...]
    kr = (hf[0:n] + hb[0:n]) * inv
    ki = (hf[n:] - hb[n:]) * inv
    v = v_ref[0]
    x = _dot(fd, v.astype(BF16))
    xr, xi = x[0:n], x[n:]
    prod = jnp.concatenate([xr * kr - xi * ki, xr * ki + xi * kr], axis=0).astype(BF16)
    y = _dot(gi_ref[...], prod)
    o_ref[0] = (y + v * bias_ref[...]) * x0_ref[0]


def _hyena_dense(vv, x0, hfb, inv, bias):
    b, l, c = vv.shape
    n = 2 * l
    cf, sf = _cis_tables(n, l, n)
    fd = jnp.concatenate([cf, -sf], axis=0).astype(BF16)
    ci, si = _cis_tables(l, n, n)
    gi = (jnp.concatenate([ci, -si], axis=1) * (1.0 / n)).astype(BF16)
    full = lambda a: pl.BlockSpec(a.shape, lambda bi: (0,) * a.ndim)
    tok = pl.BlockSpec((1, l, c), lambda bi: (bi, 0, 0))
    return pl.pallas_call(
        _dense_conv_kernel,
        out_shape=jax.ShapeDtypeStruct((b, l, c), F32),
        grid=(b,),
        in_specs=[tok, tok, full(hfb), full(inv), full(bias), full(fd), full(gi)],
        out_specs=tok,
        compiler_params=_cparams("arbitrary"),
        name="hyena_dense",
    )(vv, x0, hfb, inv, bias, fd, gi)


def _hyena(vv, x0, p):
    l = vv.shape[1]
    w1p = jnp.zeros((LANES, HY_FFN), F32).at[0:HY_EMB].set(p['hy_w1'])
    row = lambda a: a.reshape(1, -1)
    hfb, norm = _filter_call(w1p, row(p['hy_b1']), p['hy_w2'], row(p['hy_b2']), p['hy_w3'], row(p['hy_freq']),
                             row(p['hy_decay']), l, min(l, 512))
    inv = 1.0 / norm
    bias = row(p['hy_bias'])
    if l >= 8 * DFT_N2:
        return _hyena_two_stage(vv, x0, hfb, inv, bias)
    return _hyena_dense(vv, x0, hfb, inv, bias)


def _post_kernel(att_ref, lru_ref, hy_ref, x_ref, mod_ref, og_ref, wo_ref, fg_ref, rwh_ref, rwl_ref, rb_ref, cnt0_ref,
                 x1_ref, f_ref, sel_ref, gate_ref, rank_ref, cnt_ref, cnt_sc, *, ctx_row):
    row = pl.program_id(0) if ctx_row is None else ctx_row

    @pl.when((pl.program_id(0) == 0) & (pl.program_id(1) == 0))
    def _():
        cnt_sc[...] = cnt0_ref[...]

    d = x_ref.shape[2]
    tm = x_ref.shape[1]
    a_w, l_w = ATT_WIDTH, LRU_WIDTH
    o = (_dot(_rms(att_ref[0], og_ref[:, 0:a_w]).astype(BF16), wo_ref[0:a_w, :])
         + _dot(_rms(lru_ref[0], og_ref[:, a_w:a_w + l_w]).astype(BF16), wo_ref[a_w:a_w + l_w, :])
         + _dot(_rms(hy_ref[0], og_ref[:, a_w + l_w:]).astype(BF16), wo_ref[a_w + l_w:, :]))
    mod = lambda k: mod_ref[pl.ds(row, 1), k * d:(k + 1) * d]
    x1 = x_ref[0] + mod(2) * o
    x1_ref[0] = x1
    f = _rms(x1, fg_ref[...]) * (1.0 + mod(4)) + mod(3)
    f_ref[0] = f
    fh, fl = _split(f)
    logits = _dot(fh, rwh_ref[...]) + (_dot(fl, rwh_ref[...]) + _dot(fh, rwl_ref[...])) + rb_ref[...]
    lane = lax.broadcasted_iota(I32, (tm, LANES), 1)
    sel = jnp.zeros((tm, LANES), I32)
    vals = jnp.zeros((tm, LANES), F32)
    onehots = []
    for k in range(TOP_K):
        m = jnp.max(logits, axis=1, keepdims=True)
        idx = jnp.min(jnp.where(logits == m, lane, LANES), axis=1, keepdims=True)
        hit = lane == idx
        onehots.append(hit.astype(F32))
        sel = jnp.where(lane == k, idx, sel)
        vals = jnp.where(lane == k, m, vals)
        logits = jnp.where(hit, -jnp.inf, logits)
    e = jnp.where(lane < TOP_K, jnp.exp(vals - vals[:, 0:1]), 0.0)
    sel_ref[0] = sel
    gate_ref[0] = e / jnp.sum(e, axis=1, keepdims=True)
    picked = onehots[0] + onehots[1] + onehots[2] + onehots[3]
    earlier = lax.broadcasted_iota(I32, (tm, tm), 0) > lax.broadcasted_iota(I32, (tm, tm), 1)
    before = _dot(earlier.astype(BF16), picked.astype(BF16)) + cnt_sc[...]
    rank = jnp.zeros((tm, LANES), I32)
    for k in range(TOP_K):
        rk = jnp.sum(onehots[k] * before, axis=1, keepdims=True)
        rank = jnp.where(lane == k, rk.astype(I32), rank)
    rank_ref[0] = rank
    cnt_sc[...] += jnp.sum(picked, axis=0, keepdims=True)
    cnt_ref[...] = cnt_sc[...]


def _post_call(att, lru, hyo, x, mod_l, og, wo, fg, rwh, rwl, rb, cnt0, ctx_row, tm):
    b, l, d = x.shape
    full = lambda a: pl.BlockSpec(a.shape, lambda bi, i: (0,) * a.ndim)
    tok = lambda w: pl.BlockSpec((1, tm, w), lambda bi, i: (bi, i, 0))
    consts = [mod_l, og, wo, fg, rwh, rwl, rb, cnt0]
    return pl.pallas_call(
        functools.partial(_post_kernel, ctx_row=ctx_row),
        out_shape=(jax.ShapeDtypeStruct((b, l, d), F32), jax.ShapeDtypeStruct((b, l, d), F32),
                   jax.ShapeDtypeStruct((b, l, LANES), I32), jax.ShapeDtypeStruct((b, l, LANES), F32),
                   jax.ShapeDtypeStruct((b, l, LANES), I32), jax.ShapeDtypeStruct((1, LANES), F32)),
        grid=(b, l // tm),
        in_specs=[tok(ATT_WIDTH), tok(LRU_WIDTH), tok(HY_WIDTH), tok(d)] + [full(a) for a in consts],
        out_specs=(tok(d), tok(d), tok(LANES), tok(LANES), tok(LANES), pl.BlockSpec((1, LANES), lambda bi, i: (0, 0))),
        scratch_shapes=[pltpu.VMEM((1, LANES), F32)],
        compiler_params=_cparams("arbitrary", "arbitrary"),
        name="post_lat" if ctx_row is None else "post_ctx",
    )(att, lru, hyo, x, *consts)


def _route(sel, rank, counts, tm, n_tiles):
    t = sel.shape[0]
    e_ids = jnp.arange(N_EXPERTS, dtype=I32)
    tiles_per = (counts + tm - 1) // tm
    tile_end = jnp.cumsum(tiles_per)
    first_tile = tile_end - tiles_per
    grp_start = jnp.cumsum(counts) - counts
    pos = jnp.sum(jnp.where(sel[:, :, None] == e_ids, first_tile * tm, 0), axis=-1) + rank
    tile_ids = jnp.arange(n_tiles, dtype=I32)
    tile_expert = jnp.minimum(jnp.sum((tile_end[None, :] <= tile_ids[:, None]).astype(I32), axis=1), N_EXPERTS - 1)
    of_tile = lambda v: jnp.sum(jnp.where(tile_expert[:, None] == e_ids, v, 0), axis=1)
    tile_off = jnp.clip(of_tile(grp_start) + (tile_ids - of_tile(first_tile)) * tm, 0, TOP_K * t)
    n_used = tile_end[N_EXPERTS - 1:N_EXPERTS]
    tok = jnp.arange(TOP_K * t, dtype=I32) // TOP_K
    _, tok_sorted = lax.sort((sel.reshape(-1), tok), num_keys=1, is_stable=True)
    n_win = TOP_K * t // IDX_STEP + 2
    padded = jnp.zeros(((n_win + 1) * IDX_STEP,), I32).at[0:TOP_K * t].set(tok_sorted).reshape(n_win + 1, IDX_STEP)
    windows = jnp.concatenate([padded[:-1], padded[1:]], axis=1)
    return pos.astype(I32), tile_expert, tile_off.astype(I32), n_used.astype(I32), windows


def _moe_kernel(te_ref, toff_ref, nu_ref, win_hbm, f_hbm, perm_ref, w1_ref, b1g_ref, b1l_ref, w2_ref, b2_ref, o_ref,
                idx_sm, xbuf, w1g_sc, w1l_sc, w2_sc, sem_idx, sem_row):
    i = pl.program_id(0)
    tm = xbuf.shape[1]
    n_used = nu_ref[0]
    slot = i % 2

    def idx_copy(tile, s):
        win = lax.shift_right_logical(toff_ref[tile], IDX_STEP.bit_length() - 1)
        return pltpu.make_async_copy(win_hbm.at[win], idx_sm.at[s], sem_idx.at[s])

    def start_rows(tile, s, lo, hi):
        delta = toff_ref[tile] & (IDX_STEP - 1)
        for r in range(lo, hi):
            pltpu.make_async_copy(f_hbm.at[pl.ds(idx_sm[s, delta + r], 1), :], xbuf.at[s, pl.ds(r, 1), :],
                                  sem_row.at[s]).start()

    def rows_wait(s):
        pltpu.make_async_copy(f_hbm.at[pl.ds(0, tm), :], xbuf.at[s], sem_row.at[s]).wait()

    @pl.when(i == 0)
    def _():
        idx_copy(0, 0).start()
        idx_copy(0, 0).wait()
        start_rows(0, 0, 0, tm)
        idx_copy(jnp.minimum(1, n_used - 1), 1).start()

    new_expert = (i == 0) | (te_ref[i] != te_ref[jnp.maximum(i - 1, 0)])

    @pl.when(new_expert & (i < n_used))
    def _():
        for jb in range(w1_ref.shape[3] // (2 * LANES)):
            blk = w1_ref[0, 0, :, 2 * jb * LANES:2 * (jb + 1) * LANES].astype(BF16)
            de = _dot(blk, perm_ref[...])
            w1g_sc[:, jb * LANES:(jb + 1) * LANES] = de[:, 0:LANES].astype(BF16)
            w1l_sc[:, jb * LANES:(jb + 1) * LANES] = de[:, LANES:].astype(BF16)
        w2_sc[...] = w2_ref[0, 0].astype(BF16)

    nblk = 4
    grp = tm // nblk
    cb = w2_sc.shape[1] // nblk
    for s in range(2):
        @pl.when((i < n_used) & (slot == s))
        def _(s=s):
            nxt = jnp.minimum(i + 1, n_used - 1)
            idx_copy(nxt, 1 - s).wait()
            rows_wait(s)
            xb = xbuf[s].astype(BF16)
            acts = []
            for n in range(nblk):
                start_rows(nxt, 1 - s, n * grp, (n + 1) * grp)
                cols = slice(n * cb, (n + 1) * cb)
                glu = jnp.minimum(_dot(xb, w1g_sc[:, cols]) + b1g_ref[0, :, cols], SWIGLU_LIMIT)
                lin = jnp.clip(_dot(xb, w1l_sc[:, cols]) + b1l_ref[0, :, cols], -SWIGLU_LIMIT, SWIGLU_LIMIT)
                acts.append((glu * _sigmoid(SWIGLU_ALPHA * glu) * (lin + 1.0)).astype(BF16))
            act = jnp.concatenate(acts, axis=1)
            for n in range(nblk):
                cols = slice(n * cb, (n + 1) * cb)
                o_ref[:, cols] = _dot(act, w2_sc[:, cols]) + b2_ref[0, :, cols]
            idx_copy(jnp.minimum(i + 2, n_used - 1), s).start()

            @pl.when(i == n_used - 1)
            def _():
                rows_wait(1 - s)
                idx_copy(0, s).wait()

    @pl.when(i >= n_used)
    def _():
        o_ref[...] = jnp.zeros(o_ref.shape, F32)


def _expert_weights(layer, w1, b1, w2, b2):
    return layer, w1, b1[layer, :, None, 0::2], b1[layer, :, None, 1::2], w2, b2[layer, :, None, :]


def _moe_call(tile_expert, tile_off, n_used, windows, f, layer, w1, b1g, b1l, w2, b2, tm):
    n_tiles = tile_expert.shape[0]
    d = f.shape[1]
    dff = w2.shape[2]
    k = jnp.arange(2 * LANES)
    perm = (jnp.where(k % 2 == 0, k // 2, LANES + k // 2)[:, None] == k[None, :]).astype(BF16)
    wspec = lambda shape: pl.BlockSpec((1,) + shape, lambda i, te, toff, nu: (te[i], 0, 0))
    wstack = lambda shape: pl.BlockSpec((1, 1) + shape, lambda i, te, toff, nu: (layer, te[i], 0, 0))
    grid_spec = pltpu.PrefetchScalarGridSpec(
        num_scalar_prefetch=3,
        grid=(n_tiles,),
        in_specs=[pl.BlockSpec(memory_space=pl.ANY), pl.BlockSpec(memory_space=pl.ANY),
                  pl.BlockSpec(perm.shape, lambda i, te, toff, nu: (0, 0)),
                  wstack((d, 2 * dff)), wspec((1, dff)), wspec((1, dff)), wstack((dff, d)), wspec((1, d))],
        out_specs=pl.BlockSpec((tm, d), lambda i, te, toff, nu: (i, 0)),
        scratch_shapes=[pltpu.SMEM((2, IDX_WINDOW), I32), pltpu.VMEM((2, tm, d), F32),
                        pltpu.VMEM((d, dff), BF16), pltpu.VMEM((d, dff), BF16), pltpu.VMEM((dff, d), BF16),
                        pltpu.SemaphoreType.DMA((2,)), pltpu.SemaphoreType.DMA((2,))],
    )
    return pl.pallas_call(
        _moe_kernel,
        out_shape=jax.ShapeDtypeStruct((n_tiles * tm, d), F32),
        grid_spec=grid_spec,
        compiler_params=_cparams("arbitrary"),
        name="moe_experts",
    )(tile_expert, tile_off, n_used, windows, f, perm, w1, b1g, b1l, w2, b2)


def _combine_kernel(pos_hbm, y_hbm, gate_ref, x1_ref, mod_ref, o_ref, idx_sm, ybuf, sem_idx, sem_row,
                    *, tiles_per_batch, ctx_row):
    i = pl.program_id(0)
    n = pl.num_programs(0)
    tc = x1_ref.shape[0]
    d = x1_ref.shape[1]
    slot = i % 2

    def idx_copy(tile, s):
        return pltpu.make_async_copy(pos_hbm.at[tile], idx_sm.at[s], sem_idx.at[s])

    def start_rows(s):
        for r in range(TOP_K * tc):
            pltpu.make_async_copy(y_hbm.at[pl.ds(idx_sm[s, r], 1), :], ybuf.at[s, pl.ds(r, 1), :],
                                  sem_row.at[s]).start()

    @pl.when(i == 0)
    def _():
        idx_copy(0, 0).start()
        idx_copy(0, 0).wait()
        start_rows(0)

        @pl.when(n > 1)
        def _():
            idx_copy(1, 1).start()

    for s in range(2):
        @pl.when((i + 1 < n) & (slot == 1 - s))
        def _(s=s):
            idx_copy(i + 1, s).wait()
            start_rows(s)

            @pl.when(i + 2 < n)
            def _():
                idx_copy(i + 2, 1 - s).start()

    pltpu.make_async_copy(y_hbm.at[pl.ds(0, TOP_K * tc), :], ybuf.at[slot], sem_row.at[slot]).wait()
    row = ctx_row if tiles_per_batch is None else i // tiles_per_batch
    gf = mod_ref[pl.ds(row, 1), 5 * d:6 * d]
    acc = None
    for j in range(TOP_K):
        term = gate_ref[:, j:j + 1] * ybuf[slot, j * tc:(j + 1) * tc, :]
        acc = term if acc is None else acc + term
    o_ref[...] = x1_ref[...] + gf * acc


def _combine_call(pos_tiles, y, gate, x1, mod_l, tiles_per_batch, ctx_row, tc):
    t, d = x1.shape
    return pl.pallas_call(
        functools.partial(_combine_kernel, tiles_per_batch=tiles_per_batch, ctx_row=ctx_row),
        out_shape=jax.ShapeDtypeStruct((t, d), F32),
        grid=(t // tc,),
        in_specs=[pl.BlockSpec(memory_space=pl.ANY), pl.BlockSpec(memory_space=pl.ANY),
                  pl.BlockSpec((tc, LANES), lambda i: (i, 0)), pl.BlockSpec((tc, d), lambda i: (i, 0)),
                  pl.BlockSpec(mod_l.shape, lambda i: (0, 0))],
        out_specs=pl.BlockSpec((tc, d), lambda i: (i, 0)),
        scratch_shapes=[pltpu.SMEM((2, TOP_K * tc), I32), pltpu.VMEM((2, TOP_K * tc, d), F32),
                        pltpu.SemaphoreType.DMA((2,)), pltpu.SemaphoreType.DMA((2,))],
        compiler_params=_cparams("arbitrary"),
        name="moe_combine",
    )(pos_tiles, y, gate, x1, mod_l)


def _moe(f, sel, rank, counts, parts, mod_l, ew, ctx_row):
    t, d = f.shape
    tm = MOE_TILE
    tc = 128
    n_tiles = TOP_K * t // tm + N_EXPERTS
    pos, tile_expert, tile_off, n_used, windows = _route(sel[:, 0:TOP_K], rank[:, 0:TOP_K], counts, tm, n_tiles)
    y = _moe_call(tile_expert, tile_off, n_used, windows, f, *ew, tm)
    outs = []
    first = 0
    for x1, gate, seq in parts:
        tp = x1.shape[0]
        pos_tiles = pos[first:first + tp].reshape(tp // tc, tc, TOP_K).transpose(0, 2, 1).reshape(tp // tc, TOP_K * tc)
        outs.append(_combine_call(pos_tiles, y, gate, x1, mod_l, None if seq is None else seq // tc, ctx_row, tc))
        first += tp
    return outs


def _rope_tables(s):
    rows = (jnp.arange(s, dtype=I32) // GRID_W).astype(F32)
    cols = (jnp.arange(s, dtype=I32) % GRID_W).astype(F32)
    nfreq = HEAD_DIM // 4
    inv = ROPE_THETA ** (-jnp.arange(nfreq, dtype=F32) / nfreq)
    ang = jnp.concatenate([rows[:, None] * inv, cols[:, None] * inv], axis=-1)
    cos, sin = jnp.cos(ang), jnp.sin(ang)
    reps = LANES // HEAD_DIM
    return jnp.tile(jnp.concatenate([cos, cos], -1), (1, reps)), jnp.tile(jnp.concatenate([-sin, sin], -1), (1, reps))


def _pick_tk(lk):
    for k in range(768, 0, -LANES):
        if lk % k == 0:
            return k
    raise ValueError(f"{lk} keys do not split into lane-aligned chunks")


def kernel(x, c, ctx, c_ctx, w_mod, b_mod, norm_mix_g, norm_ffn_g, w_in, q_norm_g, k_norm_g, lru_conv_w, lru_conv_b, lru_gate_w, lru_gate_b, lru_lambda, hy_conv_w, hy_conv_b, hy_w1, hy_b1, hy_w2, hy_b2, hy_w3, hy_freq, hy_decay, hy_bias, out_norm_g, w_out, router_w, router_b, moe_w1, moe_b1, moe_w2, moe_b2):
    b, s, d = x.shape
    lc = ctx.shape[1]
    depth = w_mod.shape[0]
    ctx_row = b
    assert b < SUBLANES and d == D_MODEL
    c8 = jnp.concatenate([c, c_ctx[None, :], jnp.zeros((SUBLANES - b - 1, d), F32)], axis=0)
    mods = _mod_call(c8, w_mod, b_mod)
    rope_tabs = _rope_tables(s)
    lane = jnp.arange(LANES)
    bd = ((lane[:, None] // HEAD_DIM) == (lane[None, :] // HEAD_DIM)).astype(F32) * (1.0 / HEAD_DIM)
    bd = bd.astype(BF16)
    row = lambda a: a.reshape(1, -1)
    tile_heads = lambda g: jnp.tile(g.reshape(1, HEAD_DIM), (1, LANES // HEAD_DIM))
    tm = min(512, s)
    tmc = min(256, lc)
    t_lat = b * s

    for l in range(depth):
        last = l == depth - 1
        mod_l = mods[l]
        w_in_l = w_in[l].astype(BF16)
        qg, kg = tile_heads(q_norm_g[l]), tile_heads(k_norm_g[l])
        p = {'hy_w1': hy_w1[l], 'hy_b1': hy_b1[l], 'hy_w2': hy_w2[l], 'hy_b2': hy_b2[l], 'hy_w3': hy_w3[l],
             'hy_freq': hy_freq[l], 'hy_decay': hy_decay[l], 'hy_bias': hy_bias[l]}
        conv = (lru_conv_w[l], row(lru_conv_b[l]), hy_conv_w[l], row(hy_conv_b[l]))

        q, k, v, rest = _in_proj_call(x, mod_l, row(norm_mix_g[l]), w_in_l, qg, kg, bd, rope_tabs, None, tm)
        qc, kc, vc, rest_c = _in_proj_call(ctx, mod_l, row(norm_mix_g[l]), w_in_l, qg, kg, bd, None, ctx_row, tmc)
        xl, vv, x0 = _prep_call(rest, *conv, tm)
        xl_c, vv_c, x0_c = _prep_call(rest_c, *conv, tmc)

        kd = _dup_heads(jnp.concatenate([kc, k], axis=1))
        vd = _val_heads(jnp.concatenate([vc, v], axis=1))
        att = _attn_call(q, kd, vd, min(256, s), _pick_tk(lc + s))

        zeros_h = jnp.zeros((b, 1, LRU_WIDTH), F32)
        gws = [_lru_gate_weights(lru_gate_w[l, dr], lru_gate_b[l, dr]) for dr in range(2)]
        lams = [row(lru_lambda[l, dr]) for dr in range(2)]
        hc_f = _lru_call(xl_c, *gws[0], lams[0], zeros_h, False, tmc)
        hc_b, lru_c = _lru_call(xl_c, *gws[1], lams[1], zeros_h, True, tmc, hf=hc_f, rest=rest_c)
        h_f = _lru_call(xl, *gws[0], lams[0], hc_f[:, lc - 1:lc], False, tm)
        _, lru = _lru_call(xl, *gws[1], lams[1], hc_b[:, 0:1], True, tm, hf=h_f, rest=rest)

        hyo = _hyena(vv, x0, p)

        rw = jnp.zeros((d, LANES), F32).at[:, 0:N_EXPERTS].set(router_w[l])
        rwh, rwl = _split(rw)
        rb = jnp.full((1, LANES), -1e30, F32).at[0, 0:N_EXPERTS].set(router_b[l])
        post_consts = (row(out_norm_g[l]), w_out[l].astype(BF16), row(norm_ffn_g[l]), rwh, rwl, rb)
        no_picks = jnp.zeros((1, LANES), F32)
        x1, f, sel, gate, rank, cnt = _post_call(att, lru, hyo, x, mod_l, *post_consts, no_picks, None, tm)
        ew = _expert_weights(l, moe_w1, moe_b1, moe_w2, moe_b2)
        flat = lambda a: a.reshape(-1, a.shape[-1])
        counts = lambda cn: cn[0, 0:N_EXPERTS].astype(I32)
        if last:
            (x,) = _moe(flat(f), flat(sel), flat(rank), counts(cnt), [(flat(x1), flat(gate), s)], mod_l, ew, ctx_row)
            x = x.reshape(b, s, d)
        else:
            att_c = _attn_call(qc, _dup_heads(kc), _val_heads(vc), tmc, _pick_tk(lc))
            hyo_c = _hyena(vv_c, x0_c, p)
            x1c, fc, selc, gatec, rankc, cnt = _post_call(att_c, lru_c, hyo_c, ctx, mod_l, *post_consts, cnt, ctx_row,
                                                         tmc)
            cat = lambda a, bb: jnp.concatenate([flat(a), flat(bb)], axis=0)
            parts = [(flat(x1), flat(gate), s), (flat(x1c), flat(gatec), None)]
            x, ctx = _moe(cat(f, fc), cat(sel, selc), cat(rank, rankc), counts(cnt), parts, mod_l, ew, ctx_row)
            x = x.reshape(b, s, d)
            ctx = ctx.reshape(b, lc, d)
    return x
```

```python
import functools
import math

import jax
import jax.numpy as jnp
from jax import lax
from jax.experimental import pallas as pl
from jax.experimental.pallas import tpu as pltpu

F32 = jnp.float32
BF16 = jnp.bfloat16
I32 = jnp.int32

D_MODEL = 1024
HEAD_DIM = 64
ATT_WIDTH = 512
KV_WIDTH = 128
LRU_WIDTH = 256
HY_WIDTH = 256
REST_WIDTH = 2 * LRU_WIDTH + 3 * HY_WIDTH
IN_WIDTH = ATT_WIDTH + 2 * KV_WIDTH + REST_WIDTH
GRID_W = 64
ROPE_THETA = 10000.0
LRU_C = 8.0
HY_EMB = 33
HY_BANDS = 16
HY_FFN = 64
N_EXPERTS = 32
TOP_K = 4
D_FF = 1024
SWIGLU_ALPHA = 1.702
SWIGLU_LIMIT = 7.0
EPS = 1e-6
LANES = 128
SUBLANES = 8
DFT_N2 = 128
VMEM_LIMIT = 56 * 1024 * 1024
MOE_TILE = 256
IDX_STEP = 1024
IDX_WINDOW = 2 * IDX_STEP
Q_SCALE = HEAD_DIM ** -0.5 * math.log2(math.e)


def _cparams(*sem):
    return pltpu.CompilerParams(dimension_semantics=sem, vmem_limit_bytes=VMEM_LIMIT)


def _dot(a, b):
    return jnp.dot(a, b, preferred_element_type=F32)


def _split(x):
    hi = x.astype(BF16)
    return hi, (x - hi.astype(F32)).astype(BF16)


def _dot3(a, b):
    ah, al = _split(a)
    bh, bl = _split(b)
    return _dot(ah, bh) + (_dot(al, bh) + _dot(ah, bl))


def _sigmoid(x):
    return 1.0 / (1.0 + jnp.exp(-x))


def _rms(x, g):
    return x * lax.rsqrt(jnp.mean(x * x, axis=-1, keepdims=True) + EPS) * g


def _mod_kernel(c_ref, w_ref, b_ref, o_ref):
    c = c_ref[...]
    o_ref[0] = _dot3(c * _sigmoid(c), w_ref[0]) + b_ref[0]


def _mod_call(c8, w_mod, b_mod):
    depth, d, n6 = w_mod.shape
    tn = n6 // 4
    return pl.pallas_call(
        _mod_kernel,
        out_shape=jax.ShapeDtypeStruct((depth, SUBLANES, n6), F32),
        grid=(depth, n6 // tn),
        in_specs=[pl.BlockSpec((SUBLANES, d), lambda l, j: (0, 0)),
                  pl.BlockSpec((1, d, tn), lambda l, j: (l, 0, j)),
                  pl.BlockSpec((1, 1, tn), lambda l, j: (l, 0, j))],
        out_specs=pl.BlockSpec((1, SUBLANES, tn), lambda l, j: (l, 0, j)),
        compiler_params=_cparams("arbitrary", "arbitrary"),
        name="mod",
    )(c8, w_mod, b_mod.reshape(depth, 1, n6))


def _head_norm(x, g, bd):
    hi, lo = _split(x * x)
    ms = _dot(hi, bd) + _dot(lo, bd)
    return x * lax.rsqrt(ms + EPS) * g


def _rope(y, cos, sin, first_half):
    swapped = jnp.where(first_half, pltpu.roll(y, LANES - HEAD_DIM // 2, 1), pltpu.roll(y, HEAD_DIM // 2, 1))
    return y * cos + swapped * sin


def _in_proj_kernel(*refs, rope, ctx_row):
    if rope:
        x_ref, mod_ref, g_ref, w_ref, qg_ref, kg_ref, bd_ref, cos_ref, sin_ref, q_ref, k_ref, v_ref, r_ref = refs
    else:
        x_ref, mod_ref, g_ref, w_ref, qg_ref, kg_ref, bd_ref, q_ref, k_ref, v_ref, r_ref = refs
    row = pl.program_id(0) if ctx_row is None else ctx_row
    d = x_ref.shape[2]
    tm = x_ref.shape[1]
    shift = mod_ref[pl.ds(row, 1), 0:d]
    scale = mod_ref[pl.ds(row, 1), d:2 * d]
    h = _rms(x_ref[0], g_ref[...]) * (1.0 + scale) + shift
    u = _dot(h.astype(BF16), w_ref[...])
    bd = bd_ref[...]
    if rope:
        cos = cos_ref[...]
        sin = sin_ref[...]
        lane = lax.broadcasted_iota(I32, (tm, LANES), 1)
        first_half = (lane & (HEAD_DIM // 2)) == 0
    for p in range(ATT_WIDTH // LANES):
        y = _head_norm(u[:, p * LANES:(p + 1) * LANES], qg_ref[...], bd)
        if rope:
            y = _rope(y, cos, sin, first_half)
        q_ref[0, :, p * LANES:(p + 1) * LANES] = (y * Q_SCALE).astype(BF16)
    y = _head_norm(u[:, ATT_WIDTH:ATT_WIDTH + KV_WIDTH], kg_ref[...], bd)
    if rope:
        y = _rope(y, cos, sin, first_half)
    k_ref[0] = y.astype(BF16)
    v_ref[0] = u[:, ATT_WIDTH + KV_WIDTH:ATT_WIDTH + 2 * KV_WIDTH].astype(BF16)
    r_ref[0] = u[:, ATT_WIDTH + 2 * KV_WIDTH:]


def _in_proj_call(x, mod_l, g, w_in, qg, kg, bd, rope_tabs, ctx_row, tm):
    b, l, d = x.shape
    rope = rope_tabs is not None
    full = lambda shape: pl.BlockSpec(shape, lambda bi, i: (0,) * len(shape))
    in_specs = [pl.BlockSpec((1, tm, d), lambda bi, i: (bi, i, 0)),
                full(mod_l.shape), full((1, d)), full(w_in.shape), full((1, LANES)), full((1, LANES)),
                full((LANES, LANES))]
    args = [x, mod_l, g, w_in, qg, kg, bd]
    if rope:
        in_specs += [pl.BlockSpec((tm, LANES), lambda bi, i: (i, 0))] * 2
        args += list(rope_tabs)
    tok = lambda w: pl.BlockSpec((1, tm, w), lambda bi, i: (bi, i, 0))
    return pl.pallas_call(
        functools.partial(_in_proj_kernel, rope=rope, ctx_row=ctx_row),
        out_shape=(jax.ShapeDtypeStruct((b, l, ATT_WIDTH), BF16), jax.ShapeDtypeStruct((b, l, KV_WIDTH), BF16),
                   jax.ShapeDtypeStruct((b, l, KV_WIDTH), BF16), jax.ShapeDtypeStruct((b, l, REST_WIDTH), F32)),
        grid=(b, l // tm),
        in_specs=in_specs,
        out_specs=(tok(ATT_WIDTH), tok(KV_WIDTH), tok(KV_WIDTH), tok(REST_WIDTH)),
        compiler_params=_cparams("arbitrary", "arbitrary"),
        name="in_proj_lat" if rope else "in_proj_ctx",
    )(*args)


def _conv_taps(win, w, left, tm):
    acc = None
    for j in range(w.shape[0]):
        off = SUBLANES + j - left
        term = win[off:off + tm] * w[j:j + 1]
        acc = term if acc is None else acc + term
    return acc


def _prep_kernel(cur_ref, prev_ref, next_ref, lw_ref, lb_ref, hw_ref, hb_ref, xl_ref, vv_ref, x0_ref):
    i = pl.program_id(1)
    tm = cur_ref.shape[1]
    has_prev = (i > 0).astype(F32)
    has_next = (i < pl.num_programs(1) - 1).astype(F32)

    def window(lo, hi):
        return jnp.concatenate([prev_ref[0, :, lo:hi] * has_prev, cur_ref[0, :, lo:hi],
                                next_ref[0, :, lo:hi] * has_next], axis=0)

    xl_ref[0] = _conv_taps(window(0, LRU_WIDTH), lw_ref[...], 2, tm) + lb_ref[...]
    uc = _conv_taps(window(2 * LRU_WIDTH, REST_WIDTH), hw_ref[...], 1, tm) + hb_ref[...]
    x0_ref[0] = uc[:, 0:HY_WIDTH]
    vv_ref[0] = uc[:, 2 * HY_WIDTH:] * uc[:, HY_WIDTH:2 * HY_WIDTH]


def _prep_call(rest, lw, lb, hw, hb, tm):
    b, l, w = rest.shape
    nb = tm // SUBLANES
    last = l // SUBLANES - 1
    full = lambda a: pl.BlockSpec(a.shape, lambda bi, i: (0,) * a.ndim)
    tok = pl.BlockSpec((1, tm, LRU_WIDTH), lambda bi, i: (bi, i, 0))
    out = jax.ShapeDtypeStruct((b, l, LRU_WIDTH), F32)
    return pl.pallas_call(
        _prep_kernel,
        out_shape=(out, out, out),
        grid=(b, l // tm),
        in_specs=[pl.BlockSpec((1, tm, w), lambda bi, i: (bi, i, 0)),
                  pl.BlockSpec((1, SUBLANES, w), lambda bi, i: (bi, jnp.maximum(i * nb - 1, 0), 0)),
                  pl.BlockSpec((1, SUBLANES, w), lambda bi, i: (bi, jnp.minimum((i + 1) * nb, last), 0)),
                  full(lw), full(lb), full(hw), full(hb)],
        out_specs=(tok, tok, tok),
        compiler_params=_cparams("arbitrary", "arbitrary"),
        name="conv_prep",
    )(rest, rest, rest, lw, lb, hw, hb)


def _attn_kernel(q_ref, k_ref, v_ref, o_ref, qst_sc, m_sc, acc_sc, pa_sc, pb_sc, aa_sc, ab_sc, *, tk):
    tq = q_ref.shape[1]
    lk = k_ref.shape[2]
    nl = tk // LANES
    low = lax.broadcasted_iota(I32, (tq, LANES), 1) < HEAD_DIM
    for g in range(2):
        for n, p in enumerate((2 * g, 2 * g + 1)):
            qs = q_ref[0, :, p * LANES:(p + 1) * LANES]
            zero = jnp.zeros_like(qs)
            qst_sc[g, 2 * n * tq:(2 * n + 1) * tq, :] = jnp.where(low, qs, zero)
            qst_sc[g, (2 * n + 1) * tq:(2 * n + 2) * tq, :] = jnp.where(low, zero, qs)
    m_sc[...] = jnp.full(m_sc.shape, -jnp.inf, F32)
    acc_sc[...] = jnp.zeros(acc_sc.shape, F32)

    def scores(c, p_sc, a_sc):
        start = pl.multiple_of(c * tk, tk)
        for g in range(2):
            kc = k_ref[0, g, pl.ds(start, tk), :]
            s = lax.dot_general(qst_sc[g], kc, (((1,), (1,)), ((), ())), preferred_element_type=F32)
            cols = [s[:, n * LANES:(n + 1) * LANES] for n in range(nl)]
            mx = cols[0]
            for col in cols[1:]:
                mx = jnp.maximum(mx, col)
            m_old = m_sc[g]
            m_new = jnp.maximum(m_old, jnp.max(mx, axis=1, keepdims=True))
            a_sc[g] = jnp.exp2(m_old - m_new)
            p_sc[g] = jnp.concatenate([jnp.exp2(col - m_new).astype(BF16) for col in cols], axis=1)
            m_sc[g] = m_new

    def values(c, p_sc, a_sc):
        start = pl.multiple_of(c * tk, tk)
        for g in range(2):
            acc_sc[g] = a_sc[g] * acc_sc[g] + _dot(p_sc[g], v_ref[0, g, pl.ds(start, tk), :])

    nk = lk // tk
    scores(0, pa_sc, aa_sc)

    def body(i, carry):
        scores(2 * i + 1, pb_sc, ab_sc)
        values(2 * i, pa_sc, aa_sc)
        scores(2 * i + 2, pa_sc, aa_sc)
        values(2 * i + 1, pb_sc, ab_sc)
        return carry

    lax.fori_loop(0, (nk - 1) // 2, body, 0)
    if nk % 2:
        values(nk - 1, pa_sc, aa_sc)
    else:
        scores(nk - 1, pb_sc, ab_sc)
        values(nk - 2, pa_sc, aa_sc)
        values(nk - 1, pb_sc, ab_sc)
    for g in range(2):
        acc = acc_sc[g]
        o = acc / acc[:, HEAD_DIM:HEAD_DIM + 1]
        for n, p in enumerate((2 * g, 2 * g + 1)):
            o_ref[0, :, p * LANES:(p + 1) * LANES] = jnp.where(
                low, o[2 * n * tq:(2 * n + 1) * tq], pltpu.roll(o[(2 * n + 1) * tq:(2 * n + 2) * tq], HEAD_DIM, 1))


def _attn_call(q, kd, vd, tq, tk):
    b, l, _ = q.shape
    lk = kd.shape[2]
    kv = pl.BlockSpec((1, 2, lk, LANES), lambda bi, i: (bi, 0, 0, 0))
    return pl.pallas_call(
        functools.partial(_attn_kernel, tk=tk),
        out_shape=jax.ShapeDtypeStruct((b, l, ATT_WIDTH), F32),
        grid=(b, l // tq),
        in_specs=[pl.BlockSpec((1, tq, ATT_WIDTH), lambda bi, i: (bi, i, 0)), kv, kv],
        out_specs=pl.BlockSpec((1, tq, ATT_WIDTH), lambda bi, i: (bi, i, 0)),
        scratch_shapes=[pltpu.VMEM((2, 4 * tq, LANES), BF16), pltpu.VMEM((2, 4 * tq, LANES), F32),
                        pltpu.VMEM((2, 4 * tq, LANES), F32),
                        pltpu.VMEM((2, 4 * tq, tk), BF16), pltpu.VMEM((2, 4 * tq, tk), BF16),
                        pltpu.VMEM((2, 4 * tq, LANES), F32), pltpu.VMEM((2, 4 * tq, LANES), F32)],
        compiler_params=_cparams("arbitrary", "arbitrary"),
        name="attn",
    )(q, kd, vd)


def _dup_heads(x):
    halves = [jnp.concatenate([x[..., g * HEAD_DIM:(g + 1) * HEAD_DIM]] * 2, axis=-1) for g in range(2)]
    return jnp.stack(halves, axis=1)


def _val_heads(x):
    pad = jnp.zeros(x.shape[:-1] + (LANES - HEAD_DIM,), x.dtype).at[..., 0].set(1)
    return jnp.stack([jnp.concatenate([x[..., g * HEAD_DIM:(g + 1) * HEAD_DIM], pad], axis=-1) for g in range(2)], 1)


def _gelu_tanh(x):
    return 0.5 * x * (1.0 + jnp.tanh(math.sqrt(2.0 / math.pi) * (x + 0.044715 * (x * x * x))))


def _lru_kernel(*refs, reverse):
    if reverse:
        xl_ref, w_ref, b_ref, lam_ref, h0_ref, hf_ref, lg_ref, h_ref, o_ref, a_sc, b_sc, c_sc = refs
    else:
        xl_ref, w_ref, b_ref, lam_ref, h0_ref, h_ref, a_sc, b_sc, c_sc = refs
    tc = xl_ref.shape[1]
    nt = tc // SUBLANES

    @pl.when(pl.program_id(1) == 0)
    def _():
        c_sc[...] = jnp.broadcast_to(h0_ref[0], c_sc.shape)

    xc = xl_ref[0]
    gates = _dot(xc.astype(BF16), w_ref[...]) + b_ref[...]
    r = _sigmoid(gates[:, 0:LRU_WIDTH])
    gi = _sigmoid(gates[:, LRU_WIDTH:])
    nlam = -lam_ref[...]
    softplus = jnp.maximum(nlam, 0.0) + jnp.log1p(jnp.exp(-jnp.abs(nlam)))
    log_a = -LRU_C * r * softplus
    a = jnp.exp(log_a)
    a_sc[...] = a
    b_sc[...] = jnp.sqrt(1.0 - a * a) * (gi * xc)
    row = lax.broadcasted_iota(I32, (SUBLANES, LRU_WIDTH), 0)

    def tile(k, carry):
        kk = nt - 1 - k if reverse else k
        start = pl.multiple_of(kk * SUBLANES, SUBLANES)
        a = a_sc[pl.ds(start, SUBLANES), :]
        bv = b_sc[pl.ds(start, SUBLANES), :]
        for dist in (1, 2, 4):
            shift = SUBLANES - dist if reverse else dist
            keep = (row < SUBLANES - dist) if reverse else (row >= dist)
            a_n = pltpu.roll(a, shift, 0)
            b_n = pltpu.roll(bv, shift, 0)
            bv = jnp.where(keep, a * b_n + bv, bv)
            a = jnp.where(keep, a * a_n, a)
        h = a * c_sc[...] + bv
        h_ref[0, pl.ds(start, SUBLANES), :] = h
        edge = h[0:1] if reverse else h[SUBLANES - 1:SUBLANES]
        c_sc[...] = jnp.broadcast_to(edge, c_sc.shape)
        return carry

    lax.fori_loop(0, nt, tile, 0)
    if reverse:
        o_ref[0] = (hf_ref[0] + h_ref[0]) * _gelu_tanh(lg_ref[0])


def _lru_call(xl, w, bias, lam, h0, reverse, tc, hf=None, rest=None):
    b, l, c = xl.shape
    nc = l // tc
    chunk = (lambda bi, i: (bi, nc - 1 - i, 0)) if reverse else (lambda bi, i: (bi, i, 0))
    full = lambda a: pl.BlockSpec(a.shape, lambda bi, i: (0,) * a.ndim)
    tok = pl.BlockSpec((1, tc, c), chunk)
    in_specs = [tok, full(w), full(bias), full(lam), pl.BlockSpec((1, 1, c), lambda bi, i: (bi, 0, 0))]
    args = [xl, w, bias, lam, h0]
    out = jax.ShapeDtypeStruct((b, l, c), F32)
    if reverse:
        lg_chunk = (lambda bi, i: (bi, nc - 1 - i, 1))
        in_specs += [tok, pl.BlockSpec((1, tc, c), lg_chunk)]
        args += [hf, rest]
        out_shape, out_specs = (out, out), (tok, tok)
    else:
        out_shape, out_specs = out, tok
    return pl.pallas_call(
        functools.partial(_lru_kernel, reverse=reverse),
        out_shape=out_shape,
        grid=(b, nc),
        in_specs=in_specs,
        out_specs=out_specs,
        scratch_shapes=[pltpu.VMEM((tc, c), F32), pltpu.VMEM((tc, c), F32), pltpu.VMEM((SUBLANES, c), F32)],
        compiler_params=_cparams("arbitrary", "arbitrary"),
        name="lru_bwd" if reverse else "lru_fwd",
    )(*args)


def _lru_gate_weights(gw, gb):
    nb, bw = gw.shape[1], gw.shape[2]
    cols = []
    for j in range(2):
        m = jnp.zeros((nb * bw, nb * bw), F32)
        for n in range(nb):
            m = m.at[n * bw:(n + 1) * bw, n * bw:(n + 1) * bw].set(gw[j, n])
        cols.append(m)
    return jnp.concatenate(cols, axis=1).astype(BF16), gb.reshape(1, -1)


def _filter_kernel(w1_ref, b1_ref, w2_ref, b2_ref, w3_ref, fr_ref, dec_ref, h_ref, n_ref, *, seq):
    i = pl.program_id(0)
    tl = h_ref.shape[1]
    c = h_ref.shape[2]
    t = (i * tl + lax.broadcasted_iota(I32, (tl, LANES), 0)).astype(F32)
    lane = lax.broadcasted_iota(I32, (tl, LANES), 1)
    t01 = t * (1.0 / (seq - 1))
    kidx = jnp.where(lane <= HY_BANDS, lane - 1, lane - 1 - HY_BANDS).astype(F32)
    band = 1e-4 + kidx * ((HY_BANDS - 1 - 1e-4) / (HY_BANDS - 1))
    ang = (2.0 * math.pi / seq) * t * band
    z = jnp.where(lane == 0, t01,
                  jnp.where(lane <= HY_BANDS, jnp.cos(ang),
                            jnp.where(lane <= 2 * HY_BANDS, -jnp.sin(ang), 0.0)))
    fr = fr_ref[...]
    h = jnp.sin(fr * (_dot3(z, w1_ref[...]) + b1_ref[...]))
    h = jnp.sin(fr * (_dot3(h, w2_ref[...]) + b2_ref[...]))
    h = _dot3(h, w3_ref[...]) * jnp.exp(-t01[:, 0:1] * jnp.abs(dec_ref[...]))
    hf = h[:, 0:c]
    hb = jnp.where(t[:, 0:1] == 0.0, 0.0, h[:, c:])
    h_ref[0] = hf
    h_ref[1] = hb

    @pl.when(i == 0)
    def _():
        n_ref[...] = jnp.zeros(n_ref.shape, F32)

    n_ref[...] += jnp.sum(jnp.abs(hf) + jnp.abs(hb), axis=0, keepdims=True)


def _filter_call(w1p, b1, w2, b2, w3, fr, dec, seq, tl):
    c = w3.shape[1] // 2
    full = lambda a: pl.BlockSpec(a.shape, lambda i: (0,) * a.ndim)
    args = [w1p, b1, w2, b2, w3, fr, dec]
    return pl.pallas_call(
        functools.partial(_filter_kernel, seq=seq),
        out_shape=(jax.ShapeDtypeStruct((2, seq, c), F32), jax.ShapeDtypeStruct((1, c), F32)),
        grid=(seq // tl,),
        in_specs=[full(a) for a in args],
        out_specs=(pl.BlockSpec((2, tl, c), lambda i: (0, i, 0)), pl.BlockSpec((1, c), lambda i: (0, 0))),
        compiler_params=_cparams("arbitrary"),
        name="hyena_filter",
    )(*args)


def _cis_tables(na, nb, n):
    prod = (jnp.arange(na, dtype=I32)[:, None] * jnp.arange(nb, dtype=I32)[None, :]) % n
    ang = prod.astype(F32) * (2.0 * math.pi / n)
    return jnp.cos(ang), jnp.sin(ang)


def _lane_rep(x):
    return jnp.broadcast_to(x[:, :, None], x.shape + (LANES,))


def _dft_a_kernel(v_ref, f_ref, tc_ref, ts_ref, o_ref):
    n1 = tc_ref.shape[1]
    c = v_ref.shape[3]
    rep = c // LANES
    for j in range(v_ref.shape[1]):
        p_ = _dot(f_ref[...], v_ref[0, j].astype(BF16))
        pr, pi = p_[0:n1], p_[n1:]
        tc = jnp.concatenate([tc_ref[j]] * rep, axis=1)
        ts = jnp.concatenate([ts_ref[j]] * rep, axis=1)
        o_ref[0, j, 0] = (pr * tc + pi * ts).astype(BF16)
        o_ref[0, j, 1] = (pi * tc - pr * ts).astype(BF16)


def _dft_a_call(vp, f1s, tc, ts, cs2):
    nb, n2, half, c = vp.shape
    n1 = 2 * half
    full = lambda a: pl.BlockSpec(a.shape, lambda b, i: (0,) * a.ndim)
    return pl.pallas_call(
        _dft_a_kernel,
        out_shape=jax.ShapeDtypeStruct((nb, n2, 2, n1, c), BF16),
        grid=(nb, n2 // cs2),
        in_specs=[pl.BlockSpec((1, cs2, half, c), lambda b, i: (b, i, 0, 0)), full(f1s),
                  pl.BlockSpec((cs2, n1, LANES), lambda b, i: (i, 0, 0)),
                  pl.BlockSpec((cs2, n1, LANES), lambda b, i: (i, 0, 0))],
        out_specs=pl.BlockSpec((1, cs2, 2, n1, c), lambda b, i: (b, i, 0, 0, 0)),
        compiler_params=_cparams("arbitrary", "arbitrary"),
        name="dft_stage_a",
    )(vp, f1s, tc, ts)


def _spec_kernel(a_ref, mf_ref, inv_ref, o_ref):
    n2 = a_ref.shape[3]
    c = a_ref.shape[4]
    inv = inv_ref[...]
    for j in range(a_ref.shape[1]):
        yf = _dot(mf_ref[...], a_ref[0, j].reshape(2 * n2, c))
        yb = _dot(mf_ref[...], a_ref[1, j].reshape(2 * n2, c))
        o_ref[j, 0] = (yf[0:n2] + yb[0:n2]) * inv
        o_ref[j, 1] = (yf[n2:] - yb[n2:]) * inv


def _spec_call(at, mf, inv, cf):
    _, n1, _, n2, c = at.shape
    full = lambda a: pl.BlockSpec(a.shape, lambda i: (0,) * a.ndim)
    return pl.pallas_call(
        _spec_kernel,
        out_shape=jax.ShapeDtypeStruct((n1, 2, n2, c), F32),
        grid=(n1 // cf,),
        in_specs=[pl.BlockSpec((2, cf, 2, n2, c), lambda i: (0, i, 0, 0, 0)), full(mf), full(inv)],
        out_specs=pl.BlockSpec((cf, 2, n2, c), lambda i: (i, 0, 0, 0)),
        compiler_params=_cparams("arbitrary"),
        name="hyena_spectrum",
    )(at, mf, inv)


def _dft_b_kernel(a_ref, k_ref, mf_ref, mi_ref, tc_ref, ts_ref, o_ref):
    n2 = a_ref.shape[3]
    c = a_ref.shape[4]
    rep = c // LANES
    for j in range(a_ref.shape[1]):
        y = _dot(mf_ref[...], a_ref[0, j].reshape(2 * n2, c))
        yr, yi = y[0:n2], y[n2:]
        kr, ki = k_ref[j, 0], k_ref[j, 1]
        prod = jnp.concatenate([yr * kr - yi * ki, yr * ki + yi * kr], axis=0).astype(BF16)
        z = _dot(mi_ref[...], prod)
        zr, zi = z[0:n2], z[n2:]
        tc = jnp.concatenate([tc_ref[j]] * rep, axis=1)
        ts = jnp.concatenate([ts_ref[j]] * rep, axis=1)
        o_ref[0, j, 0] = (zr * tc - zi * ts).astype(BF16)
        o_ref[0, j, 1] = (zr * ts + zi * tc).astype(BF16)


def _dft_b_call(at, kspec, mf, mi, tc, ts, cf):
    nb, n1, _, n2, c = at.shape
    full = lambda a: pl.BlockSpec(a.shape, lambda b, i: (0,) * a.ndim)
    blk = pl.BlockSpec((1, cf, 2, n2, c), lambda b, i: (b, i, 0, 0, 0))
    tw = pl.BlockSpec((cf, n2, LANES), lambda b, i: (i, 0, 0))
    return pl.pallas_call(
        _dft_b_kernel,
        out_shape=jax.ShapeDtypeStruct((nb, n1, 2, n2, c), BF16),
        grid=(nb, n1 // cf),
        in_specs=[blk, pl.BlockSpec((cf, 2, n2, c), lambda b, i: (i, 0, 0, 0)), full(mf), full(mi), tw, tw],
        out_specs=blk,
        compiler_params=_cparams("arbitrary", "arbitrary"),
        name="dft_stage_b",
    )(at, kspec, mf, mi, tc, ts)


def _dft_c_kernel(z_ref, g_ref, vv_ref, x0_ref, bias_ref, o_ref):
    n1 = z_ref.shape[3]
    c = z_ref.shape[4]
    for j in range(z_ref.shape[1]):
        y = _dot(g_ref[...], z_ref[0, j].reshape(2 * n1, c))
        o_ref[0, j] = (y + vv_ref[0, j] * bias_ref[...]) * x0_ref[0, j]


def _dft_c_call(zt, g, vvp, x0p, bias, cs2):
    nb, n2, _, n1, c = zt.shape
    half = n1 // 2
    full = lambda a: pl.BlockSpec(a.shape, lambda b, i: (0,) * a.ndim)
    tok = pl.BlockSpec((1, cs2, half, c), lambda b, i: (b, i, 0, 0))
    return pl.pallas_call(
        _dft_c_kernel,
        out_shape=jax.ShapeDtypeStruct((nb, n2, half, c), F32),
        grid=(nb, n2 // cs2),
        in_specs=[pl.BlockSpec((1, cs2, 2, n1, c), lambda b, i: (b, i, 0, 0, 0)), full(g), tok, tok, full(bias)],
        out_specs=tok,
        compiler_params=_cparams("arbitrary", "arbitrary"),
        name="dft_stage_c",
    )(zt, g, vvp, x0p, bias)


def _swap_time(x, n2):
    b, l, c = x.shape
    return x.reshape(b, l // n2, n2, c).transpose(0, 2, 1, 3)


def _hyena_two_stage(vv, x0, hfb, inv, bias):
    b, l, c = vv.shape
    n2 = DFT_N2
    n = 2 * l
    n1 = n // n2
    half = n1 // 2
    cs2 = 8
    cf = 8
    c1, s1 = _cis_tables(n1, half, n1)
    f1s = jnp.concatenate([c1, -s1], axis=0).astype(BF16)
    c2, s2 = _cis_tables(n2, n2, n2)
    mf = jnp.concatenate([jnp.concatenate([c2, s2], 1), jnp.concatenate([-s2, c2], 1)], 0).astype(BF16)
    mi = jnp.concatenate([jnp.concatenate([c2, -s2], 1), jnp.concatenate([s2, c2], 1)], 0).astype(BF16)
    gc, gs = _cis_tables(half, n1, n1)
    g = (jnp.concatenate([gc, -gs], axis=1) * (1.0 / n)).astype(BF16)
    twa = [_lane_rep(t) for t in _cis_tables(n2, n1, n)]
    twb = [_lane_rep(t) for t in _cis_tables(n1, n2, n)]
    flip = lambda a: a.transpose(0, 3, 2, 1, 4)
    ka = _dft_a_call(_swap_time(hfb, n2), f1s, twa[0], twa[1], cs2)
    kspec = _spec_call(flip(ka), mf, inv, cf)
    vvp = _swap_time(vv, n2)
    a = _dft_a_call(vvp, f1s, twa[0], twa[1], cs2)
    z = _dft_b_call(flip(a), kspec, mf, mi, twb[0], twb[1], cf)
    yp = _dft_c_call(flip(z), g, vvp, _swap_time(x0, n2), bias, cs2)
    return yp.transpose(0, 2, 1, 3).reshape(b, l, c)


def _dense_conv_kernel(v_ref, x0_ref, h_ref, inv_ref, bias_ref, fd_ref, gi_ref, o_ref):
    n = fd_ref.shape[0] // 2
    fd = fd_ref[...]
    hf = _dot(fd, h_ref[0].astype(BF16))
    hb = _dot(fd, h_ref[1].astype(BF16))
    inv = inv_ref[...]
    kr = (hf[0:n] + hb[0:n]) * inv
    ki = (hf[n:] - hb[n:]) * inv
    v = v_ref[0]
    x = _dot(fd, v.astype(BF16))
    xr, xi = x[0:n], x[n:]
    prod = jnp.concatenate([xr * kr - xi * ki, xr * ki + xi * kr], axis=0).astype(BF16)
    y = _dot(gi_ref[...], prod)
    o_ref[0] = (y + v * bias_ref[...]) * x0_ref[0]


def _hyena_dense(vv, x0, hfb, inv, bias):
    b, l, c = vv.shape
    n = 2 * l
    cf, sf = _cis_tables(n, l, n)
    fd = jnp.concatenate([cf, -sf], axis=0).astype(BF16)
    ci, si = _cis_tables(l, n, n)
    gi = (jnp.concatenate([ci, -si], axis=1) * (1.0 / n)).astype(BF16)
    full = lambda a: pl.BlockSpec(a.shape, lambda bi: (0,) * a.ndim)
    tok = pl.BlockSpec((1, l, c), lambda bi: (bi, 0, 0))
    return pl.pallas_call(
        _dense_conv_kernel,
        out_shape=jax.ShapeDtypeStruct((b, l, c), F32),
        grid=(b,),
        in_specs=[tok, tok, full(hfb), full(inv), full(bias), full(fd), full(gi)],
        out_specs=tok,
        compiler_params=_cparams("arbitrary"),
        name="hyena_dense",
    )(vv, x0, hfb, inv, bias, fd, gi)


def _hyena(vv, x0, p):
    l = vv.shape[1]
    w1p = jnp.zeros((LANES, HY_FFN), F32).at[0:HY_EMB].set(p['hy_w1'])
    row = lambda a: a.reshape(1, -1)
    hfb, norm = _filter_call(w1p, row(p['hy_b1']), p['hy_w2'], row(p['hy_b2']), p['hy_w3'], row(p['hy_freq']),
                             row(p['hy_decay']), l, min(l, 512))
    inv = 1.0 / norm
    bias = row(p['hy_bias'])
    if l >= 8 * DFT_N2:
        return _hyena_two_stage(vv, x0, hfb, inv, bias)
    return _hyena_dense(vv, x0, hfb, inv, bias)


def _post_kernel(att_ref, lru_ref, hy_ref, x_ref, mod_ref, og_ref, wo_ref, fg_ref, rwh_ref, rwl_ref, rb_ref, cnt0_ref,
                 x1_ref, f_ref, sel_ref, gate_ref, rank_ref, cnt_ref, cnt_sc, *, ctx_row):
    row = pl.program_id(0) if ctx_row is None else ctx_row

    @pl.when((pl.program_id(0) == 0) & (pl.program_id(1) == 0))
    def _():
        cnt_sc[...] = cnt0_ref[...]

    d = x_ref.shape[2]
    tm = x_ref.shape[1]
    a_w, l_w = ATT_WIDTH, LRU_WIDTH
    o = (_dot(_rms(att_ref[0], og_ref[:, 0:a_w]).astype(BF16), wo_ref[0:a_w, :])
         + _dot(_rms(lru_ref[0], og_ref[:, a_w:a_w + l_w]).astype(BF16), wo_ref[a_w:a_w + l_w, :])
         + _dot(_rms(hy_ref[0], og_ref[:, a_w + l_w:]).astype(BF16), wo_ref[a_w + l_w:, :]))
    mod = lambda k: mod_ref[pl.ds(row, 1), k * d:(k + 1) * d]
    x1 = x_ref[0] + mod(2) * o
    x1_ref[0] = x1
    f = _rms(x1, fg_ref[...]) * (1.0 + mod(4)) + mod(3)
    f_ref[0] = f
    fh, fl = _split(f)
    logits = _dot(fh, rwh_ref[...]) + (_dot(fl, rwh_ref[...]) + _dot(fh, rwl_ref[...])) + rb_ref[...]
    lane = lax.broadcasted_iota(I32, (tm, LANES), 1)
    sel = jnp.zeros((tm, LANES), I32)
    vals = jnp.zeros((tm, LANES), F32)
    onehots = []
    for k in range(TOP_K):
        m = jnp.max(logits, axis=1, keepdims=True)
        idx = jnp.min(jnp.where(logits == m, lane, LANES), axis=1, keepdims=True)
        hit = lane == idx
        onehots.append(hit.astype(F32))
        sel = jnp.where(lane == k, idx, sel)
        vals = jnp.where(lane == k, m, vals)
        logits = jnp.where(hit, -jnp.inf, logits)
    e = jnp.where(lane < TOP_K, jnp.exp(vals - vals[:, 0:1]), 0.0)
    sel_ref[0] = sel
    gate_ref[0] = e / jnp.sum(e, axis=1, keepdims=True)
    picked = onehots[0] + onehots[1] + onehots[2] + onehots[3]
    earlier = lax.broadcasted_iota(I32, (tm, tm), 0) > lax.broadcasted_iota(I32, (tm, tm), 1)
    before = _dot(earlier.astype(BF16), picked.astype(BF16)) + cnt_sc[...]
    rank = jnp.zeros((tm, LANES), I32)
    for k in range(TOP_K):
        rk = jnp.sum(onehots[k] * before, axis=1, keepdims=True)
        rank = jnp.where(lane == k, rk.astype(I32), rank)
    rank_ref[0] = rank
    cnt_sc[...] += jnp.sum(picked, axis=0, keepdims=True)
    cnt_ref[...] = cnt_sc[...]


def _post_call(att, lru, hyo, x, mod_l, og, wo, fg, rwh, rwl, rb, cnt0, ctx_row, tm):
    b, l, d = x.shape
    full = lambda a: pl.BlockSpec(a.shape, lambda bi, i: (0,) * a.ndim)
    tok = lambda w: pl.BlockSpec((1, tm, w), lambda bi, i: (bi, i, 0))
    consts = [mod_l, og, wo, fg, rwh, rwl, rb, cnt0]
    return pl.pallas_call(
        functools.partial(_post_kernel, ctx_row=ctx_row),
        out_shape=(jax.ShapeDtypeStruct((b, l, d), F32), jax.ShapeDtypeStruct((b, l, d), F32),
                   jax.ShapeDtypeStruct((b, l, LANES), I32), jax.ShapeDtypeStruct((b, l, LANES), F32),
                   jax.ShapeDtypeStruct((b, l, LANES), I32), jax.ShapeDtypeStruct((1, LANES), F32)),
        grid=(b, l // tm),
        in_specs=[tok(ATT_WIDTH), tok(LRU_WIDTH), tok(HY_WIDTH), tok(d)] + [full(a) for a in consts],
        out_specs=(tok(d), tok(d), tok(LANES), tok(LANES), tok(LANES), pl.BlockSpec((1, LANES), lambda bi, i: (0, 0))),
        scratch_shapes=[pltpu.VMEM((1, LANES), F32)],
        compiler_params=_cparams("arbitrary", "arbitrary"),
        name="post_lat" if ctx_row is None else "post_ctx",
    )(att, lru, hyo, x, *consts)


def _route(sel, rank, counts, tm, n_tiles):
    t = sel.shape[0]
    e_ids = jnp.arange(N_EXPERTS, dtype=I32)
    tiles_per = (counts + tm - 1) // tm
    tile_end = jnp.cumsum(tiles_per)
    first_tile = tile_end - tiles_per
    grp_start = jnp.cumsum(counts) - counts
    pos = jnp.sum(jnp.where(sel[:, :, None] == e_ids, first_tile * tm, 0), axis=-1) + rank
    tile_ids = jnp.arange(n_tiles, dtype=I32)
    tile_expert = jnp.minimum(jnp.sum((tile_end[None, :] <= tile_ids[:, None]).astype(I32), axis=1), N_EXPERTS - 1)
    of_tile = lambda v: jnp.sum(jnp.where(tile_expert[:, None] == e_ids, v, 0), axis=1)
    tile_off = jnp.clip(of_tile(grp_start) + (tile_ids - of_tile(first_tile)) * tm, 0, TOP_K * t)
    n_used = tile_end[N_EXPERTS - 1:N_EXPERTS]
    tok = jnp.arange(TOP_K * t, dtype=I32) // TOP_K
    _, tok_sorted = lax.sort((sel.reshape(-1), tok), num_keys=1, is_stable=True)
    n_win = TOP_K * t // IDX_STEP + 2
    padded = jnp.zeros(((n_win + 1) * IDX_STEP,), I32).at[0:TOP_K * t].set(tok_sorted).reshape(n_win + 1, IDX_STEP)
    windows = jnp.concatenate([padded[:-1], padded[1:]], axis=1)
    return pos.astype(I32), tile_expert, tile_off.astype(I32), n_used.astype(I32), windows


def _moe_kernel(te_ref, toff_ref, nu_ref, win_hbm, f_hbm, perm_ref, w1_ref, b1g_ref, b1l_ref, w2_ref, b2_ref, o_ref,
                idx_a, idx_b, xbuf, w1g_sc, w1l_sc, w2_sc, sem_idx, sem_row):
    i = pl.program_id(0)
    tm = xbuf.shape[1]
    n_used = nu_ref[0]
    slot = i % 2
    idx_sm = (idx_a, idx_b)

    def idx_copy(tile, s):
        win = lax.shift_right_logical(toff_ref[tile], IDX_STEP.bit_length() - 1)
        return pltpu.make_async_copy(win_hbm.at[win], idx_sm[s], sem_idx.at[s])

    def start_rows(tile, s):
        delta = toff_ref[tile] & (IDX_STEP - 1)
        for r in range(tm):
            pltpu.make_async_copy(f_hbm.at[pl.ds(idx_sm[s][delta + r], 1), :], xbuf.at[s, pl.ds(r, 1), :],
                                  sem_row.at[s]).start()

    @pl.when(i == 0)
    def _():
        idx_copy(0, 0).start()
        idx_copy(0, 0).wait()
        start_rows(0, 0)

        @pl.when(n_used > 1)
        def _():
            idx_copy(1, 1).start()

    for s in range(2):
        @pl.when((i + 1 < n_used) & (slot == 1 - s))
        def _(s=s):
            idx_copy(i + 1, s).wait()
            start_rows(i + 1, s)

            @pl.when(i + 2 < n_used)
            def _():
                idx_copy(i + 2, 1 - s).start()

    new_expert = (i == 0) | (te_ref[i] != te_ref[jnp.maximum(i - 1, 0)])

    @pl.when(new_expert & (i < n_used))
    def _():
        for jb in range(w1_ref.shape[3] // (2 * LANES)):
            blk = w1_ref[0, 0, :, 2 * jb * LANES:2 * (jb + 1) * LANES].astype(BF16)
            de = _dot(blk, perm_ref[...])
            w1g_sc[:, jb * LANES:(jb + 1) * LANES] = de[:, 0:LANES].astype(BF16)
            w1l_sc[:, jb * LANES:(jb + 1) * LANES] = de[:, LANES:].astype(BF16)
        w2_sc[...] = w2_ref[0, 0].astype(BF16)

    @pl.when(i < n_used)
    def _():
        pltpu.make_async_copy(f_hbm.at[pl.ds(0, tm), :], xbuf.at[slot], sem_row.at[slot]).wait()
        xb = xbuf[slot].astype(BF16)
        glu = jnp.minimum(_dot(xb, w1g_sc[...]) + b1g_ref[0], SWIGLU_LIMIT)
        lin = jnp.clip(_dot(xb, w1l_sc[...]) + b1l_ref[0], -SWIGLU_LIMIT, SWIGLU_LIMIT)
        act = glu * _sigmoid(SWIGLU_ALPHA * glu) * (lin + 1.0)
        o_ref[...] = _dot(act.astype(BF16), w2_sc[...]) + b2_ref[0]

    @pl.when(i >= n_used)
    def _():
        o_ref[...] = jnp.zeros(o_ref.shape, F32)


def _expert_weights(layer, w1, b1, w2, b2):
    return layer, w1, b1[layer, :, None, 0::2], b1[layer, :, None, 1::2], w2, b2[layer, :, None, :]


def _moe_call(tile_expert, tile_off, n_used, windows, f, layer, w1, b1g, b1l, w2, b2, tm):
    n_tiles = tile_expert.shape[0]
    d = f.shape[1]
    dff = w2.shape[2]
    k = jnp.arange(2 * LANES)
    perm = (jnp.where(k % 2 == 0, k // 2, LANES + k // 2)[:, None] == k[None, :]).astype(BF16)
    wspec = lambda shape: pl.BlockSpec((1,) + shape, lambda i, te, toff, nu: (te[i], 0, 0))
    wstack = lambda shape: pl.BlockSpec((1, 1) + shape, lambda i, te, toff, nu: (layer, te[i], 0, 0))
    grid_spec = pltpu.PrefetchScalarGridSpec(
        num_scalar_prefetch=3,
        grid=(n_tiles,),
        in_specs=[pl.BlockSpec(memory_space=pl.ANY), pl.BlockSpec(memory_space=pl.ANY),
                  pl.BlockSpec(perm.shape, lambda i, te, toff, nu: (0, 0)),
                  wstack((d, 2 * dff)), wspec((1, dff)), wspec((1, dff)), wstack((dff, d)), wspec((1, d))],
        out_specs=pl.BlockSpec((tm, d), lambda i, te, toff, nu: (i, 0)),
        scratch_shapes=[pltpu.SMEM((IDX_WINDOW,), I32), pltpu.SMEM((IDX_WINDOW,), I32), pltpu.VMEM((2, tm, d), F32),
                        pltpu.VMEM((d, dff), BF16), pltpu.VMEM((d, dff), BF16), pltpu.VMEM((dff, d), BF16),
                        pltpu.SemaphoreType.DMA((2,)), pltpu.SemaphoreType.DMA((2,))],
    )
    return pl.pallas_call(
        _moe_kernel,
        out_shape=jax.ShapeDtypeStruct((n_tiles * tm, d), F32),
        grid_spec=grid_spec,
        compiler_params=_cparams("arbitrary"),
        name="moe_experts",
    )(tile_expert, tile_off, n_used, windows, f, perm, w1, b1g, b1l, w2, b2)


def _combine_kernel(pos_hbm, y_hbm, gate_ref, x1_ref, mod_ref, o_ref, idx_sm, ybuf, sem_idx, sem_row,
                    *, tiles_per_batch, ctx_row):
    i = pl.program_id(0)
    n = pl.num_programs(0)
    tc = x1_ref.shape[0]
    d = x1_ref.shape[1]
    slot = i % 2

    def idx_copy(tile, s):
        return pltpu.make_async_copy(pos_hbm.at[tile], idx_sm.at[s], sem_idx.at[s])

    def start_rows(s):
        for r in range(TOP_K * tc):
            pltpu.make_async_copy(y_hbm.at[pl.ds(idx_sm[s, r], 1), :], ybuf.at[s, pl.ds(r, 1), :],
                                  sem_row.at[s]).start()

    @pl.when(i == 0)
    def _():
        idx_copy(0, 0).start()
        idx_copy(0, 0).wait()
        start_rows(0)

        @pl.when(n > 1)
        def _():
            idx_copy(1, 1).start()

    for s in range(2):
        @pl.when((i + 1 < n) & (slot == 1 - s))
        def _(s=s):
            idx_copy(i + 1, s).wait()
            start_rows(s)

            @pl.when(i + 2 < n)
            def _():
                idx_copy(i + 2, 1 - s).start()

    pltpu.make_async_copy(y_hbm.at[pl.ds(0, TOP_K * tc), :], ybuf.at[slot], sem_row.at[slot]).wait()
    row = ctx_row if tiles_per_batch is None else i // tiles_per_batch
    gf = mod_ref[pl.ds(row, 1), 5 * d:6 * d]
    acc = None
    for j in range(TOP_K):
        term = gate_ref[:, j:j + 1] * ybuf[slot, j * tc:(j + 1) * tc, :]
        acc = term if acc is None else acc + term
    o_ref[...] = x1_ref[...] + gf * acc


def _combine_call(pos_tiles, y, gate, x1, mod_l, tiles_per_batch, ctx_row, tc):
    t, d = x1.shape
    return pl.pallas_call(
        functools.partial(_combine_kernel, tiles_per_batch=tiles_per_batch, ctx_row=ctx_row),
        out_shape=jax.ShapeDtypeStruct((t, d), F32),
        grid=(t // tc,),
        in_specs=[pl.BlockSpec(memory_space=pl.ANY), pl.BlockSpec(memory_space=pl.ANY),
                  pl.BlockSpec((tc, LANES), lambda i: (i, 0)), pl.BlockSpec((tc, d), lambda i: (i, 0)),
                  pl.BlockSpec(mod_l.shape, lambda i: (0, 0))],
        out_specs=pl.BlockSpec((tc, d), lambda i: (i, 0)),
        scratch_shapes=[pltpu.SMEM((2, TOP_K * tc), I32), pltpu.VMEM((2, TOP_K * tc, d), F32),
                        pltpu.SemaphoreType.DMA((2,)), pltpu.SemaphoreType.DMA((2,))],
        compiler_params=_cparams("arbitrary"),
        name="moe_combine",
    )(pos_tiles, y, gate, x1, mod_l)


def _moe(f, sel, rank, counts, parts, mod_l, ew, ctx_row):
    t, d = f.shape
    tm = MOE_TILE
    tc = 128
    n_tiles = TOP_K * t // tm + N_EXPERTS
    pos, tile_expert, tile_off, n_used, windows = _route(sel[:, 0:TOP_K], rank[:, 0:TOP_K], counts, tm, n_tiles)
    y = _moe_call(tile_expert, tile_off, n_used, windows, f, *ew, tm)
    outs = []
    first = 0
    for x1, gate, seq in parts:
        tp = x1.shape[0]
        pos_tiles = pos[first:first + tp].reshape(tp // tc, tc, TOP_K).transpose(0, 2, 1).reshape(tp // tc, TOP_K * tc)
        outs.append(_combine_call(pos_tiles, y, gate, x1, mod_l, None if seq is None else seq // tc, ctx_row, tc))
        first += tp
    return outs


def _rope_tables(s):
    rows = (jnp.arange(s, dtype=I32) // GRID_W).astype(F32)
    cols = (jnp.arange(s, dtype=I32) % GRID_W).astype(F32)
    nfreq = HEAD_DIM // 4
    inv = ROPE_THETA ** (-jnp.arange(nfreq, dtype=F32) / nfreq)
    ang = jnp.concatenate([rows[:, None] * inv, cols[:, None] * inv], axis=-1)
    cos, sin = jnp.cos(ang), jnp.sin(ang)
    reps = LANES // HEAD_DIM
    return jnp.tile(jnp.concatenate([cos, cos], -1), (1, reps)), jnp.tile(jnp.concatenate([-sin, sin], -1), (1, reps))


def _pick_tk(lk):
    for k in range(768, 0, -LANES):
        if lk % k == 0:
            return k
    raise ValueError(f"{lk} keys do not split into lane-aligned chunks")


def kernel(x, c, ctx, c_ctx, w_mod, b_mod, norm_mix_g, norm_ffn_g, w_in, q_norm_g, k_norm_g, lru_conv_w, lru_conv_b, lru_gate_w, lru_gate_b, lru_lambda, hy_conv_w, hy_conv_b, hy_w1, hy_b1, hy_w2, hy_b2, hy_w3, hy_freq, hy_decay, hy_bias, out_norm_g, w_out, router_w, router_b, moe_w1, moe_b1, moe_w2, moe_b2):
    b, s, d = x.shape
    lc = ctx.shape[1]
    depth = w_mod.shape[0]
    ctx_row = b
    assert b < SUBLANES and d == D_MODEL
    c8 = jnp.concatenate([c, c_ctx[None, :], jnp.zeros((SUBLANES - b - 1, d), F32)], axis=0)
    mods = _mod_call(c8, w_mod, b_mod)
    rope_tabs = _rope_tables(s)
    lane = jnp.arange(LANES)
    bd = ((lane[:, None] // HEAD_DIM) == (lane[None, :] // HEAD_DIM)).astype(F32) * (1.0 / HEAD_DIM)
    bd = bd.astype(BF16)
    row = lambda a: a.reshape(1, -1)
    tile_heads = lambda g: jnp.tile(g.reshape(1, HEAD_DIM), (1, LANES // HEAD_DIM))
    tm = min(512, s)
    tmc = min(256, lc)
    t_lat = b * s

    for l in range(depth):
        last = l == depth - 1
        mod_l = mods[l]
        w_in_l = w_in[l].astype(BF16)
        qg, kg = tile_heads(q_norm_g[l]), tile_heads(k_norm_g[l])
        p = {'hy_w1': hy_w1[l], 'hy_b1': hy_b1[l], 'hy_w2': hy_w2[l], 'hy_b2': hy_b2[l], 'hy_w3': hy_w3[l],
             'hy_freq': hy_freq[l], 'hy_decay': hy_decay[l], 'hy_bias': hy_bias[l]}
        conv = (lru_conv_w[l], row(lru_conv_b[l]), hy_conv_w[l], row(hy_conv_b[l]))

        q, k, v, rest = _in_proj_call(x, mod_l, row(norm_mix_g[l]), w_in_l, qg, kg, bd, rope_tabs, None, tm)
        qc, kc, vc, rest_c = _in_proj_call(ctx, mod_l, row(norm_mix_g[l]), w_in_l, qg, kg, bd, None, ctx_row, tmc)
        xl, vv, x0 = _prep_call(rest, *conv, tm)
        xl_c, vv_c, x0_c = _prep_call(rest_c, *conv, tmc)

        kd = _dup_heads(jnp.concatenate([kc, k], axis=1))
        vd = _val_heads(jnp.concatenate([vc, v], axis=1))
        att = _attn_call(q, kd, vd, min(256, s), _pick_tk(lc + s))

        zeros_h = jnp.zeros((b, 1, LRU_WIDTH), F32)
        gws = [_lru_gate_weights(lru_gate_w[l, dr], lru_gate_b[l, dr]) for dr in range(2)]
        lams = [row(lru_lambda[l, dr]) for dr in range(2)]
        hc_f = _lru_call(xl_c, *gws[0], lams[0], zeros_h, False, tmc)
        hc_b, lru_c = _lru_call(xl_c, *gws[1], lams[1], zeros_h, True, tmc, hf=hc_f, rest=rest_c)
        h_f = _lru_call(xl, *gws[0], lams[0], hc_f[:, lc - 1:lc], False, tm)
        _, lru = _lru_call(xl, *gws[1], lams[1], hc_b[:, 0:1], True, tm, hf=h_f, rest=rest)

        hyo = _hyena(vv, x0, p)

        rw = jnp.zeros((d, LANES), F32).at[:, 0:N_EXPERTS].set(router_w[l])
        rwh, rwl = _split(rw)
        rb = jnp.full((1, LANES), -1e30, F32).at[0, 0:N_EXPERTS].set(router_b[l])
        post_consts = (row(out_norm_g[l]), w_out[l].astype(BF16), row(norm_ffn_g[l]), rwh, rwl, rb)
        no_picks = jnp.zeros((1, LANES), F32)
        x1, f, sel, gate, rank, cnt = _post_call(att, lru, hyo, x, mod_l, *post_consts, no_picks, None, tm)
        ew = _expert_weights(l, moe_w1, moe_b1, moe_w2, moe_b2)
        flat = lambda a: a.reshape(-1, a.shape[-1])
        counts = lambda cn: cn[0, 0:N_EXPERTS].astype(I32)
        if last:
            (x,) = _moe(flat(f), flat(sel), flat(rank), counts(cnt), [(flat(x1), flat(gate), s)], mod_l, ew, ctx_row)
            x = x.reshape(b, s, d)
        else:
            att_c = _attn_call(qc, _dup_heads(kc), _val_heads(vc), tmc, _pick_tk(lc))
            hyo_c = _hyena(vv_c, x0_c, p)
            x1c, fc, selc, gatec, rankc, cnt = _post_call(att_c, lru_c, hyo_c, ctx, mod_l, *post_consts, cnt, ctx_row,
                                                         tmc)
            cat = lambda a, bb: jnp.concatenate([flat(a), flat(bb)], axis=0)
            parts = [(flat(x1), flat(gate), s), (flat(x1c), flat(gatec), None)]
            x, ctx = _moe(cat(f, fc), cat(sel, selc), cat(rank, rankc), counts(cnt), parts, mod_l, ew, ctx_row)
            x = x.reshape(b, s, d)
            ctx = ctx.reshape(b, lc, d)
    return x
```
